```python
import math
import jax, jax.numpy as jnp
from jax import lax
import numpy as np

D_MODEL = 2048
BATCH = 16
SEQ = 256
DEPTH = 4
DEC_BATCH = 8
DEC_SEQ = 1024
PAST_LEN = 256

GRID_W = 64
N_EVEN = (DEPTH + 1) // 2
N_ODD = DEPTH // 2
N_MOD = 6
EPS = 1e-6
HY_CH = D_MODEL // 2
HY_ORDER = 2
HY_SHORT = 3
HY_EMB = 33
HY_FILT_HID = 64
HY_FAST_PCT = 0.3
HY_SLOW_PCT = 1.5
HY_TARGET = 1e-2
DA_WIDTH = D_MODEL - HY_CH
DA_DH = 64
DA_VD = 2 * DA_DH
DA_HEADS = DA_WIDTH // DA_VD
ROPE_THETA = 10000.0
Q_BLOCK = 128
IN_E = 3 * HY_CH + 3 * DA_WIDTH
MIX_W = HY_CH + DA_WIDTH
CV_K = 31
D_FF = 5632
N_EXPERTS = 8
TOP_K = 2
E_FF = 7168

kernel_name = 'hybrid_hyena_diffattn_conformer_prefix_step'


def rmsnorm(x, g):
    xf = x.astype(jnp.float32)
    y = xf * lax.rsqrt(jnp.mean(xf * xf, axis=-1, keepdims=True) + EPS)
    return (y * g.astype(jnp.float32)).astype(x.dtype)


def layernorm(x, g, b):
    xf = x.astype(jnp.float32)
    mu = jnp.mean(xf, axis=-1, keepdims=True)
    var = jnp.mean(jnp.square(xf - mu), axis=-1, keepdims=True)
    y = (xf - mu) * lax.rsqrt(var + EPS)
    return (y * g.astype(jnp.float32) + b.astype(jnp.float32)).astype(x.dtype)


def modulate(x, shift, scale):
    return x * (1 + scale[..., None, :]) + shift[..., None, :]


def dwconv(x, w, b):
    pad = w.shape[0] // 2
    y = lax.conv_general_dilated(x, w[:, None, :].astype(x.dtype), window_strides=(1,),
                                 padding=[(pad, pad)], dimension_numbers=('NWC', 'WIO', 'NWC'),
                                 feature_group_count=x.shape[-1])
    return y + b


def hyena_filter_freq(L, w1, b1, fr1, w2, b2, fr2, w3):
    bands = (HY_EMB - 1) // 2
    pos = jnp.arange(L, dtype=jnp.float32)
    t = jnp.linspace(0.0, 1.0, L, dtype=jnp.float32)
    f = jnp.linspace(1e-4, bands - 1, bands, dtype=jnp.float32)
    ang = 2.0 * math.pi * pos[:, None] * f[None, :] / L
    z = jnp.concatenate([t[:, None], jnp.cos(ang), -jnp.sin(ang)], axis=-1)
    h = jnp.sin(fr1 * (z @ w1 + b1))
    h = jnp.sin(fr2 * (h @ w2 + b2))
    h = (h @ w3).astype(jnp.float32).reshape(L, 2, HY_ORDER, HY_CH)
    max_decay = math.log(HY_TARGET) / HY_FAST_PCT
    min_decay = math.log(HY_TARGET) / HY_SLOW_PCT
    deltas = jnp.abs(jnp.linspace(min_decay, max_decay, HY_CH, dtype=jnp.float32))
    decay = jnp.exp(-t[:, None] * deltas[None, :])
    h = h * decay[:, None, None, :]
    two_sided = jnp.concatenate([h[:, 0], jnp.zeros((1, HY_ORDER, HY_CH), jnp.float32), h[:0:-1, 1]], axis=0)
    two_sided = two_sided / jnp.sum(jnp.abs(two_sided), axis=0, keepdims=True)
    return jnp.fft.rfft(two_sided, axis=0)


def fft_conv(z, hf):
    L = z.shape[1]
    zf = jnp.fft.rfft(z.astype(jnp.float32), n=2 * L, axis=1)
    return jnp.fft.irfft(zf * hf[None], n=2 * L, axis=1)[:, :L].astype(z.dtype)


def hyena_mixer(u, short_w, short_b, w1, b1, fr1, w2, b2, fr2, w3, fbias):
    L = u.shape[1]
    u = dwconv(u, short_w, short_b)
    v, x1, x2 = jnp.split(u, 3, axis=-1)
    hf = hyena_filter_freq(L, w1, b1, fr1, w2, b2, fr2, w3)
    z = x1 * (fft_conv(v, hf[:, 0]) + fbias[0] * v)
    return x2 * (fft_conv(z, hf[:, 1]) + fbias[1] * z)


def rope_2d(x, rows):
    half = DA_DH // 2
    nf = half // 2
    row = jnp.repeat(jnp.arange(rows), GRID_W).astype(jnp.float32)
    col = jnp.tile(jnp.arange(GRID_W), rows).astype(jnp.float32)
    inv = ROPE_THETA ** (-jnp.arange(nf, dtype=jnp.float32) / nf)

    def rot(xp, p):
        ang = p[:, None] * inv[None, :]
        cos = jnp.cos(ang)[:, None, None, :]
        sin = jnp.sin(ang)[:, None, None, :]
        a, b = xp[..., :nf], xp[..., nf:]
        return jnp.concatenate([a * cos - b * sin, b * cos + a * sin], axis=-1)

    return jnp.concatenate([rot(x[..., :half], row), rot(x[..., half:], col)], axis=-1).astype(x.dtype)


def lambda_init(layer):
    return 0.8 - 0.6 * math.exp(-0.3 * layer)


def diff_attention(q, k, v, lam):
    B, Lq = q.shape[0], q.shape[1]
    nblk = Lq // Q_BLOCK
    qb = jnp.moveaxis(q.reshape(B, nblk, Q_BLOCK, DA_HEADS, 2, DA_DH), 1, 0)
    scale = DA_DH ** -0.5

    def block(qblk):
        s = jnp.einsum('bqhmd,bkhmd->bmhqk', qblk, k).astype(jnp.float32) * scale
        p = jax.nn.softmax(s, axis=-1)
        a = p[:, 0] - lam * p[:, 1]
        return jnp.einsum('bhqk,bkhe->bqhe', a.astype(v.dtype), v)

    out = lax.map(block, qb)
    return jnp.moveaxis(out, 0, 1).reshape(B, Lq, DA_HEADS, DA_VD)


def swiglu(h, w1, w3, w2):
    return (jax.nn.silu(h @ w1) * (h @ w3)) @ w2


def moe_swiglu(h, router, w1, w3, w2):
    logits = (h @ router).astype(jnp.float32)
    top_v, top_i = lax.top_k(logits, TOP_K)
    gates = jax.nn.softmax(top_v, axis=-1)
    dense_gate = jnp.sum(jax.nn.one_hot(top_i, N_EXPERTS, dtype=jnp.float32) * gates[..., None], axis=-2)
    y = jnp.zeros_like(h)
    for e in range(N_EXPERTS):
        y = y + dense_gate[..., e:e + 1].astype(h.dtype) * swiglu(h, w1[e], w3[e], w2[e])
    return y


def conformer_conv(h, w1, b1, dw, dwb, ln_g, ln_b, w2, b2):
    a, g = jnp.split(h @ w1 + b1, 2, axis=-1)
    u = a * jax.nn.sigmoid(g)
    u = dwconv(u, dw, dwb)
    u = layernorm(u, ln_g, ln_b)
    return jax.nn.silu(u) @ w2 + b2


def even_mixer(h, layer, ck, cv, p):
    i = layer // 2
    B, L = h.shape[0], h.shape[1]
    proj = h @ p['w_in'][i]
    hy_u, q, k, v = jnp.split(proj, [3 * HY_CH, 3 * HY_CH + DA_WIDTH, 3 * HY_CH + 2 * DA_WIDTH], axis=-1)
    y_hy = hyena_mixer(hy_u, p['hy_short_w'][i], p['hy_short_b'][i], p['hy_w1'][i], p['hy_b1'][i],
                       p['hy_fr1'][i], p['hy_w2'][i], p['hy_b2'][i], p['hy_fr2'][i], p['hy_w3'][i],
                       p['hy_fbias'][i])
    q = q.reshape(B, L, DA_HEADS, 2, DA_DH)
    k = k.reshape(B, L, DA_HEADS, 2, DA_DH)
    v = v.reshape(B, L, DA_HEADS, DA_VD)
    k_state = k.reshape(B, L, DA_HEADS, 2 * DA_DH)
    if ck is None:
        k_all, v_all = k, v
    else:
        rows = L // GRID_W
        q = rope_2d(q, rows)
        k = rope_2d(k, rows)
        P = ck.shape[1]
        k_all = jnp.concatenate([ck.reshape(B, P, DA_HEADS, 2, DA_DH).astype(k.dtype), k], axis=1)
        v_all = jnp.concatenate([cv.astype(v.dtype), v], axis=1)
    li = lambda_init(layer)
    f32 = jnp.float32
    lam = (jnp.exp(jnp.sum(p['da_lq1'][i].astype(f32) * p['da_lk1'][i].astype(f32)))
           - jnp.exp(jnp.sum(p['da_lq2'][i].astype(f32) * p['da_lk2'][i].astype(f32))) + li)
    o = diff_attention(q, k_all, v_all, lam)
    o = rmsnorm(o, p['da_norm_g'][i]) * (1.0 - li)
    y = jnp.concatenate([y_hy, o.reshape(B, L, DA_WIDTH)], axis=-1) @ p['w_out'][i]
    return y, k_state, v


def trunk(x, cond, cache_k, cache_v, p):
    is_ctx = cache_k is None
    silu_c = jax.nn.silu(cond)
    ks, vs = [], []
    for layer in range(DEPTH):
        i = layer // 2
        mod = silu_c @ p['w_mod'][layer] + p['b_mod'][layer]
        sh1, sc1, g1, sh2, sc2, g2 = jnp.split(mod, N_MOD, axis=-1)
        h = modulate(rmsnorm(x, p['norm_mix_g'][layer]), sh1, sc1)
        if layer % 2 == 0:
            ck = None if is_ctx else cache_k[:, i]
            cv = None if is_ctx else cache_v[:, i]
            y, k_st, v_st = even_mixer(h, layer, ck, cv, p)
            if is_ctx:
                ks.append(k_st)
                vs.append(v_st)
        else:
            y = conformer_conv(h, p['cv_w1'][i], p['cv_b1'][i], p['cv_dw'][i], p['cv_dwb'][i],
                               p['cv_ln_g'][i], p['cv_ln_b'][i], p['cv_w2'][i], p['cv_b2'][i])
        x = x + g1[..., None, :] * y
        h = modulate(rmsnorm(x, p['norm_ffn_g'][layer]), sh2, sc2)
        if layer % 2 == 0:
            f = swiglu(h, p['ff_w1'][i], p['ff_w3'][i], p['ff_w2'][i])
        else:
            f = moe_swiglu(h, p['moe_router'][i], p['moe_w1'][i], p['moe_w3'][i], p['moe_w2'][i])
        x = x + g2[..., None, :] * f
    y = rmsnorm(x, p['final_g'])
    if is_ctx:
        return y, jnp.stack(ks, axis=1), jnp.stack(vs, axis=1)
    return y, None, None


def setup_inputs(seed: int = 0) -> dict:
    key = jax.random.key(seed)
    ks = iter(jax.random.split(key, 64))

    def nrm(shape, scale):
        return jax.random.normal(next(ks), shape, jnp.float32) * scale

    def gain(shape):
        return 1.0 + nrm(shape, 0.1)

    D = D_MODEL
    return {
        'x_prompt': nrm((BATCH, SEQ, D), 1.0),
        'x_sample': nrm((DEC_BATCH, DEC_SEQ, D), 1.0),
        'cache_k': nrm((DEC_BATCH, N_EVEN, PAST_LEN, DA_HEADS, 2 * DA_DH), 1.0),
        'cache_v': nrm((DEC_BATCH, N_EVEN, PAST_LEN, DA_HEADS, DA_VD), 1.0),
        'c': nrm((DEC_BATCH, D), 1.0),
        'c_ctx': nrm((D,), 1.0),
        'norm_mix_g': gain((DEPTH, D)),
        'norm_ffn_g': gain((DEPTH, D)),
        'w_mod': nrm((DEPTH, D, N_MOD * D), 0.5 * D ** -0.5),
        'b_mod': nrm((DEPTH, N_MOD * D), 0.02),
        'w_in': nrm((N_EVEN, D, IN_E), D ** -0.5),
        'w_out': nrm((N_EVEN, MIX_W, D), MIX_W ** -0.5),
        'hy_short_w': nrm((N_EVEN, HY_SHORT, 3 * HY_CH), HY_SHORT ** -0.5),
        'hy_short_b': nrm((N_EVEN, 3 * HY_CH), 0.02),
        'hy_w1': nrm((N_EVEN, HY_EMB, HY_FILT_HID), HY_EMB ** -0.5),
        'hy_b1': nrm((N_EVEN, HY_FILT_HID), 0.02),
        'hy_fr1': gain((N_EVEN, HY_FILT_HID)),
        'hy_w2': nrm((N_EVEN, HY_FILT_HID, HY_FILT_HID), HY_FILT_HID ** -0.5),
        'hy_b2': nrm((N_EVEN, HY_FILT_HID), 0.02),
        'hy_fr2': gain((N_EVEN, HY_FILT_HID)),
        'hy_w3': nrm((N_EVEN, HY_FILT_HID, 2 * HY_ORDER * HY_CH), HY_FILT_HID ** -0.5),
        'hy_fbias': nrm((N_EVEN, HY_ORDER, HY_CH), 0.1),
        'da_lq1': nrm((N_EVEN, DA_DH), 0.1),
        'da_lk1': nrm((N_EVEN, DA_DH), 0.1),
        'da_lq2': nrm((N_EVEN, DA_DH), 0.1),
        'da_lk2': nrm((N_EVEN, DA_DH), 0.1),
        'da_norm_g': gain((N_EVEN, DA_VD)),
        'cv_w1': nrm((N_ODD, D, 2 * D), D ** -0.5),
        'cv_b1': nrm((N_ODD, 2 * D), 0.02),
        'cv_dw': nrm((N_ODD, CV_K, D), CV_K ** -0.5),
        'cv_dwb': nrm((N_ODD, D), 0.02),
        'cv_ln_g': gain((N_ODD, D)),
        'cv_ln_b': nrm((N_ODD, D), 0.02),
        'cv_w2': nrm((N_ODD, D, D), D ** -0.5),
        'cv_b2': nrm((N_ODD, D), 0.02),
        'ff_w1': nrm((N_EVEN, D, D_FF), D ** -0.5),
        'ff_w3': nrm((N_EVEN, D, D_FF), D ** -0.5),
        'ff_w2': nrm((N_EVEN, D_FF, D), D_FF ** -0.5),
        'moe_router': nrm((N_ODD, D, N_EXPERTS), D ** -0.5),
        'moe_w1': nrm((N_ODD, N_EXPERTS, D, E_FF), D ** -0.5),
        'moe_w3': nrm((N_ODD, N_EXPERTS, D, E_FF), D ** -0.5),
        'moe_w2': nrm((N_ODD, N_EXPERTS, E_FF, D), E_FF ** -0.5),
        'final_g': gain((D,)),
    }


def reference(x_prompt, x_sample, cache_k, cache_v, c, c_ctx, norm_mix_g, norm_ffn_g, w_mod, b_mod,
              w_in, w_out, hy_short_w, hy_short_b, hy_w1, hy_b1, hy_fr1, hy_w2, hy_b2, hy_fr2, hy_w3,
              hy_fbias, da_lq1, da_lk1, da_lq2, da_lk2, da_norm_g, cv_w1, cv_b1, cv_dw, cv_dwb,
              cv_ln_g, cv_ln_b, cv_w2, cv_b2, ff_w1, ff_w3, ff_w2, moe_router, moe_w1, moe_w3, moe_w2,
              final_g):
    p = {
        'norm_mix_g': norm_mix_g, 'norm_ffn_g': norm_ffn_g, 'w_mod': w_mod, 'b_mod': b_mod,
        'w_in': w_in, 'w_out': w_out, 'hy_short_w': hy_short_w, 'hy_short_b': hy_short_b,
        'hy_w1': hy_w1, 'hy_b1': hy_b1, 'hy_fr1': hy_fr1, 'hy_w2': hy_w2, 'hy_b2': hy_b2,
        'hy_fr2': hy_fr2, 'hy_w3': hy_w3, 'hy_fbias': hy_fbias, 'da_lq1': da_lq1, 'da_lk1': da_lk1,
        'da_lq2': da_lq2, 'da_lk2': da_lk2, 'da_norm_g': da_norm_g, 'cv_w1': cv_w1, 'cv_b1': cv_b1,
        'cv_dw': cv_dw, 'cv_dwb': cv_dwb, 'cv_ln_g': cv_ln_g, 'cv_ln_b': cv_ln_b, 'cv_w2': cv_w2,
        'cv_b2': cv_b2, 'ff_w1': ff_w1, 'ff_w3': ff_w3, 'ff_w2': ff_w2, 'moe_router': moe_router,
        'moe_w1': moe_w1, 'moe_w3': moe_w3, 'moe_w2': moe_w2, 'final_g': final_g,
    }
    y_prompt, new_cache_k, new_cache_v = trunk(x_prompt, c_ctx, None, None, p)
    y_sample = trunk(x_sample, c, cache_k, cache_v, p)[0]
    return (y_prompt, y_sample, new_cache_k, new_cache_v)
```

```python
import functools
import math

import jax
import jax.numpy as jnp
from jax import lax
from jax.experimental import pallas as pl
from jax.experimental.pallas import tpu as pltpu

F32 = jnp.float32
BF16 = jnp.bfloat16
HIGHEST = lax.Precision.HIGHEST

EPS = 1e-6
GRID_W = 64
HY_EMB = 33
HY_FILT_HID = 64
HY_FAST_PCT = 0.3
HY_SLOW_PCT = 1.5
HY_TARGET = 1e-2
DA_DH = 64
DA_VD = 2 * DA_DH
ROPE_THETA = 10000.0
TOP_K = 2
MOD_ROWS = 16
LANES = 128
V7X_VMEM_LIMIT = 56 * 2**20
HALO = 16


def _cp(sem, big=False):
    return pltpu.CompilerParams(dimension_semantics=sem,
                                vmem_limit_bytes=V7X_VMEM_LIMIT if big else None)


def _sds(shape, dtype):
    return jax.ShapeDtypeStruct(shape, dtype)


def _fit(n, pref):
    best = LANES
    for w in range(LANES, min(n, pref) + 1, LANES):
        if n % w == 0:
            best = w
    return best


def _mod_row(row0, t_ctx, l_lat):
    return jnp.where(row0 < t_ctx, 0, 1 + (row0 - t_ctx) // l_lat)


def _mod_body(c_ref, w_ref, b_ref, o_ref):
    c = c_ref[...]
    s = (c * jax.nn.sigmoid(c)).astype(BF16)
    o_ref[0] = jnp.dot(s, w_ref[0].astype(BF16), preferred_element_type=F32) + b_ref[0]


def _mods(cond, w_mod, b_mod):
    depth, d, nm = w_mod.shape
    tn = _fit(nm, 1024)
    return pl.pallas_call(
        _mod_body,
        grid=(depth, nm // tn),
        in_specs=[pl.BlockSpec((MOD_ROWS, d), lambda l, n: (0, 0)),
                  pl.BlockSpec((1, d, tn), lambda l, n: (l, 0, n)),
                  pl.BlockSpec((1, 1, tn), lambda l, n: (l, 0, n))],
        out_specs=pl.BlockSpec((1, MOD_ROWS, tn), lambda l, n: (l, 0, n)),
        out_shape=_sds((depth, MOD_ROWS, nm), F32),
        compiler_params=_cp(("arbitrary", "arbitrary"), big=True),
        name="modulation",
    )(cond, w_mod, b_mod.reshape(depth, 1, nm))


def _rms(x, g):
    return x * lax.rsqrt(jnp.mean(x * x, axis=-1, keepdims=True) + EPS) * g


def _norm_mod_body(x_ref, g_ref, sh_ref, sc_ref, o_ref):
    y = _rms(x_ref[...], g_ref[0])
    o_ref[...] = (y * (1.0 + sc_ref[0]) + sh_ref[0]).astype(o_ref.dtype)


def _norm_router_body(x_ref, g_ref, sh_ref, sc_ref, r_ref, h_ref, meta_ref, *, n_exp):
    y = _rms(x_ref[...], g_ref[0])
    h = y * (1.0 + sc_ref[0]) + sh_ref[0]
    h_ref[...] = h
    logits = jnp.dot(h, r_ref[...], precision=HIGHEST, preferred_element_type=F32)
    lane = lax.broadcasted_iota(jnp.int32, logits.shape, 1)
    neg = jnp.float32(-jnp.inf)
    logits = jnp.where(lane < n_exp, logits, neg)
    m1 = jnp.max(logits, axis=-1, keepdims=True)
    i1 = jnp.min(jnp.where(logits == m1, lane, LANES), axis=-1, keepdims=True)
    rest = jnp.where(lane == i1, neg, logits)
    m2 = jnp.max(rest, axis=-1, keepdims=True)
    i2 = jnp.min(jnp.where(rest == m2, lane, LANES), axis=-1, keepdims=True)
    e2 = jnp.exp(m2 - m1)
    g1 = 1.0 / (1.0 + e2)
    g2 = e2 * g1
    meta = jnp.where(lane == 0, i1.astype(F32),
                     jnp.where(lane == 1, i2.astype(F32),
                               jnp.where(lane == 2, g1, jnp.where(lane == 3, g2, 0.0))))
    meta_ref[...] = meta


def _final_norm_body(x_ref, g_ref, o_ref):
    o_ref[...] = _rms(x_ref[...], g_ref[...])


def _mod_spec(rt, d, layer, chunk, dims):
    t_ctx, l_lat = dims
    return pl.BlockSpec(
        (1, 1, d), lambda i: (layer * MOD_ROWS + _mod_row(i * rt, t_ctx, l_lat), 0, chunk))


def _norm_mod(x, g_all, layer, modr, sh_chunk, sc_chunk, dims, rt):
    t, d = x.shape
    return pl.pallas_call(
        _norm_mod_body,
        grid=(t // rt,),
        in_specs=[pl.BlockSpec((rt, d), lambda i: (i, 0)),
                  pl.BlockSpec((1, 1, d), lambda i: (layer, 0, 0)),
                  _mod_spec(rt, d, layer, sh_chunk, dims),
                  _mod_spec(rt, d, layer, sc_chunk, dims)],
        out_specs=pl.BlockSpec((rt, d), lambda i: (i, 0)),
        out_shape=_sds((t, d), BF16),
        compiler_params=_cp(("arbitrary",)),
        name="norm_modulate",
    )(x, g_all.reshape(-1, 1, d), modr, modr)


def _norm_router(x, g_all, layer, modr, sh_chunk, sc_chunk, router_pad, n_exp, dims, rt):
    t, d = x.shape
    return pl.pallas_call(
        functools.partial(_norm_router_body, n_exp=n_exp),
        grid=(t // rt,),
        in_specs=[pl.BlockSpec((rt, d), lambda i: (i, 0)),
                  pl.BlockSpec((1, 1, d), lambda i: (layer, 0, 0)),
                  _mod_spec(rt, d, layer, sh_chunk, dims),
                  _mod_spec(rt, d, layer, sc_chunk, dims),
                  pl.BlockSpec((d, LANES), lambda i: (0, 0))],
        out_specs=[pl.BlockSpec((rt, d), lambda i: (i, 0)),
                   pl.BlockSpec((rt, LANES), lambda i: (i, 0))],
        out_shape=[_sds((t, d), F32), _sds((t, LANES), F32)],
        compiler_params=_cp(("arbitrary",)),
        name="norm_router",
    )(x, g_all.reshape(-1, 1, d), modr, modr, router_pad)


def _final_norm(x, g, rt):
    t, d = x.shape
    return pl.pallas_call(
        _final_norm_body,
        grid=(t // rt,),
        in_specs=[pl.BlockSpec((rt, d), lambda i: (i, 0)),
                  pl.BlockSpec((1, d), lambda i: (0, 0))],
        out_specs=pl.BlockSpec((rt, d), lambda i: (i, 0)),
        out_shape=_sds((t, d), F32),
        compiler_params=_cp(("arbitrary",)),
        name="final_norm",
    )(x, g.reshape(1, d))


def _mm_body(*refs, nw, has_bias, epi, grouped, has_res):
    it = iter(refs)
    if grouped:
        te_ref = next(it)
        na_ref = next(it)
    x_ref = next(it)
    w_refs = [next(it) for _ in range(nw)]
    b_refs = [next(it) for _ in range(nw)] if has_bias else []
    if has_res:
        res_ref = next(it)
        gate_ref = next(it)
    o_ref = next(it)
    wc_refs = [next(it) for _ in range(nw)]

    i = pl.program_id(1)
    if grouped:
        recast = (i == 0) | (te_ref[i] != te_ref[jnp.maximum(i - 1, 0)])
    else:
        recast = i == 0

    @pl.when(recast)
    def _():
        for w_ref, wc in zip(w_refs, wc_refs):
            wc[...] = w_ref[0].astype(BF16)

    def compute():
        x = x_ref[...].astype(BF16)
        accs = [jnp.dot(x, wc[...], preferred_element_type=F32) for wc in wc_refs]
        if has_bias:
            accs = [a + b[0] for a, b in zip(accs, b_refs)]
        if epi == "plain":
            y = accs[0]
        elif epi == "swiglu":
            y = accs[0] * jax.nn.sigmoid(accs[0]) * accs[1]
        elif epi == "glu":
            y = accs[0] * jax.nn.sigmoid(accs[1])
        if has_res:
            y = res_ref[...] + gate_ref[0] * y
        o_ref[...] = y.astype(o_ref.dtype)

    if grouped:
        pl.when(i < na_ref[0])(compute)

        @pl.when(i >= na_ref[0])
        def _():
            o_ref[...] = jnp.zeros(o_ref.shape, o_ref.dtype)
    else:
        compute()


def _matmul(x, ws, widx, col0s, n_out, *, tm, tn, epi="plain", biases=None, bidx=0,
            res=None, gate=None, out_dtype=F32, group=None, name="matmul"):
    m, k = x.shape
    nw = len(ws)
    grouped = group is not None
    has_bias = biases is not None
    has_res = res is not None
    tn = _fit(n_out, tn)
    grid = (n_out // tn, m // tm)

    if grouped:
        te, na, base = group

        def wmap(c0):
            return lambda n, i, te_r, na_r: (base + te_r[i], 0, c0 // tn + n)

        def rmap(f):
            return lambda n, i, te_r, na_r: f(n, i)

        xmap = lambda n, i, te_r, na_r: (jnp.minimum(i, na_r[0] - 1), 0)
    else:
        def wmap(c0):
            return lambda n, i: (widx, 0, c0 // tn + n)

        def rmap(f):
            return f

        xmap = lambda n, i: (i, 0)

    in_specs = [pl.BlockSpec((tm, k), xmap)]
    args = [x]
    for w, c0 in zip(ws, col0s):
        in_specs.append(pl.BlockSpec((1, k, tn), wmap(c0)))
        args.append(w)
    if has_bias:
        for b, c0 in zip(biases, col0s):
            in_specs.append(pl.BlockSpec((1, 1, tn), rmap(
                lambda n, i, c0=c0: (bidx, 0, c0 // tn + n))))
            args.append(b)
    if has_res:
        modr, layer, chunk, (t_ctx, l_lat) = gate
        in_specs.append(pl.BlockSpec((tm, tn), rmap(lambda n, i: (i, n))))
        args.append(res)
        in_specs.append(pl.BlockSpec((1, 1, tn), rmap(
            lambda n, i: (layer * MOD_ROWS + _mod_row(i * tm, t_ctx, l_lat), 0,
                          chunk * (n_out // tn) + n))))
        args.append(modr)
    out_spec = pl.BlockSpec((tm, tn), rmap(lambda n, i: (i, n)))
    scratch = [pltpu.VMEM((k, tn), BF16) for _ in range(nw)]
    body = functools.partial(_mm_body, nw=nw, has_bias=has_bias, epi=epi, grouped=grouped,
                             has_res=has_res)
    if grouped:
        grid_spec = pltpu.PrefetchScalarGridSpec(
            num_scalar_prefetch=2, grid=grid, in_specs=in_specs, out_specs=out_spec,
            scratch_shapes=scratch)
        return pl.pallas_call(body, grid_spec=grid_spec, out_shape=_sds((m, n_out), out_dtype),
                              compiler_params=_cp(("arbitrary", "arbitrary"), big=True),
                              name=name)(te, na, *args)
    return pl.pallas_call(body, grid=grid, in_specs=in_specs, out_specs=out_spec,
                          out_shape=_sds((m, n_out), out_dtype), scratch_shapes=scratch,
                          compiler_params=_cp(("arbitrary", "arbitrary"), big=True),
                          name=name)(*args)


def _dft_basis(r, s, L):
    n = 2 * L
    f = jnp.where(r < L, r, r - L)
    k = (f * s + jnp.where(r < L, 0, n // 4)) & (n - 1)
    val = jnp.cos(k.astype(F32) * (2.0 * math.pi / n))
    nyq = (1 - 2 * (s & 1)).astype(F32)
    return jnp.where(r == L, nyq, val)


def _dft_body(f_ref, g_ref, *, L, rb):
    i = pl.program_id(0)
    r = i * rb + lax.broadcasted_iota(jnp.int32, (rb, L), 0)
    s = lax.broadcasted_iota(jnp.int32, (rb, L), 1)
    f_ref[...] = _dft_basis(r, s, L)
    t = lax.broadcasted_iota(jnp.int32, (L, rb), 0)
    r2 = i * rb + lax.broadcasted_iota(jnp.int32, (L, rb), 1)
    scale = jnp.where((r2 == 0) | (r2 == L), 1.0 / (2 * L), 1.0 / L)
    g_ref[...] = _dft_basis(r2, t, L) * scale


def _dft_tables(L):
    rb = min(256, L)
    return pl.pallas_call(
        functools.partial(_dft_body, L=L, rb=rb),
        grid=(2 * L // rb,),
        out_specs=[pl.BlockSpec((rb, L), lambda i: (i, 0)),
                   pl.BlockSpec((L, rb), lambda i: (0, i))],
        out_shape=[_sds((2 * L, L), F32), _sds((L, 2 * L), F32)],
        compiler_params=_cp(("arbitrary",)),
        name="dft_tables",
    )()


def _filter_body(z_ref, w1_ref, b1_ref, fr1_ref, w2_ref, b2_ref, fr2_ref, w3a_ref, w3b_ref,
                 dl_ref, f_ref, o_ref, *, L):
    z = z_ref[...]
    dot = functools.partial(jnp.dot, precision=HIGHEST, preferred_element_type=F32)
    h = jnp.sin(fr1_ref[...] * (dot(z, w1_ref[0]) + b1_ref[...]))
    h = jnp.sin(fr2_ref[...] * (dot(h, w2_ref[0]) + b2_ref[...]))
    t = z[:, 0:1]
    decay = jnp.exp(-t * dl_ref[...])
    hf = dot(h, w3a_ref[0]) * decay
    hb = dot(h, w3b_ref[0]) * decay
    row = lax.broadcasted_iota(jnp.int32, (L, 1), 0)
    hb = jnp.where(row == 0, 0.0, hb)
    nrm = (jnp.sum(jnp.abs(hf), axis=0, keepdims=True)
           + jnp.sum(jnp.abs(hb), axis=0, keepdims=True))
    p = dot(f_ref[0:L, :], hf + hb)
    q = dot(f_ref[L:2 * L, :], hf - hb)
    sign = (1 - 2 * (row & 1)).astype(F32)
    nyq_fix = 2.0 * jnp.sum(sign * hb, axis=0, keepdims=True)
    q = q + jnp.where(row == 0, nyq_fix, 0.0)
    inv = 1.0 / nrm
    o_ref[0, 0:L, :] = p * inv
    o_ref[0, L:2 * L, :] = q * inv


def _pad2(a, rows, cols):
    return jnp.pad(a, ((0, rows - a.shape[0]), (0, cols - a.shape[1])))


def _hyena_filters(L, li, hy_w1, hy_b1, hy_fr1, hy_w2, hy_b2, hy_fr2, hy_w3, f_tab, hy_ch):
    bands = (HY_EMB - 1) // 2
    pos = jnp.arange(L, dtype=F32)
    t = jnp.linspace(0.0, 1.0, L, dtype=F32)
    f = jnp.linspace(1e-4, bands - 1, bands, dtype=F32)
    ang = 2.0 * math.pi * pos[:, None] * f[None, :] / L
    z = jnp.concatenate([t[:, None], jnp.cos(ang), -jnp.sin(ang)], axis=-1)
    z = _pad2(z, L, LANES)
    max_decay = math.log(HY_TARGET) / HY_FAST_PCT
    min_decay = math.log(HY_TARGET) / HY_SLOW_PCT
    deltas = jnp.abs(jnp.linspace(min_decay, max_decay, hy_ch, dtype=F32)).reshape(1, hy_ch)
    n_even = hy_w1.shape[0]
    w1 = jnp.pad(hy_w1, ((0, 0), (0, LANES - HY_EMB), (0, LANES - HY_FILT_HID)))
    w2 = jnp.pad(hy_w2, ((0, 0), (0, LANES - HY_FILT_HID), (0, LANES - HY_FILT_HID)))
    w3 = jnp.pad(hy_w3, ((0, 0), (0, LANES - HY_FILT_HID), (0, 0)))
    padv = lambda a: _pad2(a[li:li + 1], 1, LANES)
    cb = min(256, hy_ch)
    ncb = hy_ch // cb
    vec = pl.BlockSpec((1, LANES), lambda j: (0, 0))
    sq = pl.BlockSpec((1, LANES, LANES), lambda j: (li, 0, 0))
    return pl.pallas_call(
        functools.partial(_filter_body, L=L),
        grid=(2 * ncb,),
        in_specs=[pl.BlockSpec((L, LANES), lambda j: (0, 0)),
                  sq, vec, vec, sq, vec, vec,
                  pl.BlockSpec((1, LANES, cb), lambda j: (li, 0, j)),
                  pl.BlockSpec((1, LANES, cb), lambda j: (li, 0, 2 * ncb + j)),
                  pl.BlockSpec((1, cb), lambda j: (0, j % ncb)),
                  pl.BlockSpec((2 * L, L), lambda j: (0, 0))],
        out_specs=pl.BlockSpec((1, 2 * L, cb), lambda j: (j // ncb, 0, j % ncb)),
        out_shape=_sds((2, 2 * L, hy_ch), F32),
        compiler_params=_cp(("arbitrary",), big=True),
        name="hyena_filters",
    )(z, w1, padv(hy_b1), padv(hy_fr1), w2, padv(hy_b2), padv(hy_fr2), w3, w3, deltas, f_tab)


def _hyena_body(ymix_ref, v_ref, x1_ref, x2_ref, swv_ref, sw1_ref, sw2_ref, sbv_ref, sb1_ref,
                sb2_ref, fb_ref, hs_ref, f_ref, g_ref, o_ref, *, L):
    del ymix_ref
    row = lax.broadcasted_iota(jnp.int32, (L, 1), 0)

    def short(x_ref, w_ref, b_ref):
        x = x_ref[...]
        w = w_ref[0]
        xm = jnp.where(row == 0, 0.0, pltpu.roll(x, 1, 0))
        xp = jnp.where(row == L - 1, 0.0, pltpu.roll(x, L - 1, 0))
        return xm * w[0:1] + x * w[1:2] + xp * w[2:3] + b_ref[0]

    def fconv(z, order):
        zf = jnp.dot(f_ref[...], z.astype(BF16), preferred_element_type=F32)
        p, q = zf[0:L], zf[L:2 * L]
        hp, hq = hs_ref[order, 0:L, :], hs_ref[order, L:2 * L, :]
        yr = jnp.where(row == 0, p * hp, p * hp - q * hq)
        yq = jnp.where(row == 0, q * hq, p * hq + q * hp)
        return (jnp.dot(g_ref[:, 0:L], yr.astype(BF16), preferred_element_type=F32)
                + jnp.dot(g_ref[:, L:2 * L], yq.astype(BF16), preferred_element_type=F32))

    v = short(v_ref, swv_ref, sbv_ref)
    x1 = short(x1_ref, sw1_ref, sb1_ref)
    x2 = short(x2_ref, sw2_ref, sb2_ref)
    fb = fb_ref[0]
    z1 = x1 * (fconv(v, 0) + fb[0:1] * v)
    o_ref[...] = (x2 * (fconv(z1, 1) + fb[1:2] * z1)).astype(o_ref.dtype)


def _hyena(proj, ymix, li, nseq, L, row_blk0, hy_short_w, hy_short_b, hy_fbias, hs, f_bf, g_bf,
           hy_ch):
    t, d = ymix.shape
    cb = min(256, hy_ch)
    ncb = hy_ch // cb
    rowmap = lambda c: (lambda j, b: (row_blk0 + b, c * ncb + j))
    wmap = lambda c: (lambda j, b: (li, 0, c * ncb + j))
    sb = hy_short_b.reshape(hy_short_b.shape[0], 1, -1)
    in_specs = [pl.BlockSpec(memory_space=pl.ANY),
                pl.BlockSpec((L, cb), rowmap(0)), pl.BlockSpec((L, cb), rowmap(1)),
                pl.BlockSpec((L, cb), rowmap(2)),
                pl.BlockSpec((1, 3, cb), wmap(0)), pl.BlockSpec((1, 3, cb), wmap(1)),
                pl.BlockSpec((1, 3, cb), wmap(2)),
                pl.BlockSpec((1, 1, cb), wmap(0)), pl.BlockSpec((1, 1, cb), wmap(1)),
                pl.BlockSpec((1, 1, cb), wmap(2)),
                pl.BlockSpec((1, 2, cb), lambda j, b: (li, 0, j)),
                pl.BlockSpec((2, 2 * L, cb), lambda j, b: (0, 0, j)),
                pl.BlockSpec((2 * L, L), lambda j, b: (0, 0)),
                pl.BlockSpec((L, 2 * L), lambda j, b: (0, 0))]
    args = [ymix, proj, proj, proj, hy_short_w, hy_short_w, hy_short_w, sb, sb, sb, hy_fbias, hs,
            f_bf, g_bf]
    return pl.pallas_call(
        functools.partial(_hyena_body, L=L),
        grid=(ncb, nseq),
        in_specs=in_specs,
        out_specs=pl.BlockSpec((L, cb), lambda j, b: (row_blk0 + b, j)),
        out_shape=_sds((t, d), BF16),
        input_output_aliases={0: 0},
        compiler_params=_cp(("arbitrary", "arbitrary"), big=True),
        name="hyena_mixer",
    )(*args)


def _rope_tables(L):
    half = DA_DH // 2
    nf = half // 2
    lane = jnp.arange(LANES)
    sub = lane % DA_DH
    idx = sub % half
    inv = ROPE_THETA ** (-(idx % nf).astype(F32) / nf)
    pos = jnp.arange(L)
    p = jnp.where((sub // half)[None, :] == 0, (pos // GRID_W)[:, None], (pos % GRID_W)[:, None])
    ang = p.astype(F32) * inv[None, :]
    sgn = jnp.where(idx < nf, -1.0, 1.0)[None, :]
    return jnp.cos(ang), jnp.sin(ang) * sgn


def _attn_body(*refs, li, cached, tq):
    it = iter(refs)
    next(it)
    lq1, lk1, lq2, lk2 = [next(it) for _ in range(4)]
    q_ref, k_ref, v_ref = next(it), next(it), next(it)
    if cached:
        ck_ref, cv_ref, cq_ref, sq_ref, ckk_ref, skk_ref = [next(it) for _ in range(6)]
    g_ref = next(it)
    o_ref = next(it)

    lam = (jnp.exp(jnp.sum(lq1[...] * lk1[...], keepdims=True))
           - jnp.exp(jnp.sum(lq2[...] * lk2[...], keepdims=True)) + li)
    lane = lax.broadcasted_iota(jnp.int32, (1, LANES), 1)
    first = lane < DA_DH
    nf = DA_DH // 4

    def rope(x, c_ref, s_ref):
        swap = jnp.where((lane % (2 * nf)) < nf, pltpu.roll(x, LANES - nf, 1), pltpu.roll(x, nf, 1))
        return x * c_ref[...] + swap * s_ref[...]

    q = q_ref[...]
    k = k_ref[...]
    if cached:
        q = rope(q, cq_ref, sq_ref)
        k = rope(k, ckk_ref, skk_ref)
    q = q * (DA_DH ** -0.5)
    qs = [jnp.where(first, q, 0.0).astype(BF16), jnp.where(first, 0.0, q).astype(BF16)]
    nt = (((1,), (1,)), ((), ()))
    kb = k.astype(BF16)
    vb = v_ref[...].astype(BF16)
    if cached:
        ckb = ck_ref[0, 0].astype(BF16)
        cvb = cv_ref[0, 0].astype(BF16)

    def probs(qm):
        s = lax.dot_general(qm, kb, nt, preferred_element_type=F32)
        m = jnp.max(s, axis=-1, keepdims=True)
        if cached:
            sc = lax.dot_general(qm, ckb, nt, preferred_element_type=F32)
            m = jnp.maximum(m, jnp.max(sc, axis=-1, keepdims=True))
            ec = jnp.exp(sc - m)
        e = jnp.exp(s - m)
        den = jnp.sum(e, axis=-1, keepdims=True)
        if cached:
            den = den + jnp.sum(ec, axis=-1, keepdims=True)
            return e / den, ec / den
        return e / den, None

    p1, pc1 = probs(qs[0])
    p2, pc2 = probs(qs[1])
    o = jnp.dot((p1 - lam * p2).astype(BF16), vb, preferred_element_type=F32)
    if cached:
        o = o + jnp.dot((pc1 - lam * pc2).astype(BF16), cvb, preferred_element_type=F32)
    o = _rms(o, g_ref[0]) * (1.0 - li)
    o_ref[...] = o.astype(o_ref.dtype)


def _attention(proj, ymix, li, layer, nseq, L, row0, lams, da_norm_g, hy_ch, da_width,
               cache=None):
    t, d = ymix.shape
    heads = da_width // DA_VD
    cached = cache is not None
    tq = min(512, L)
    nq = L // tq
    qc0 = 3 * hy_ch // LANES
    kc0 = qc0 + da_width // LANES
    vc0 = kc0 + da_width // LANES
    lam_spec = pl.BlockSpec((1, DA_DH), lambda b, h, qi: (0, 0))
    in_specs = [pl.BlockSpec(memory_space=pl.ANY)] + [lam_spec] * 4 + [
        pl.BlockSpec((tq, LANES), lambda b, h, qi: (row0 // tq + b * nq + qi, qc0 + h)),
        pl.BlockSpec((L, LANES), lambda b, h, qi: (row0 // L + b, kc0 + h)),
        pl.BlockSpec((L, LANES), lambda b, h, qi: (row0 // L + b, vc0 + h))]
    args = [ymix] + [a[li:li + 1] for a in lams] + [proj, proj, proj]
    if cached:
        ck, cv, cos_t, sin_t = cache
        past = ck.shape[2]
        in_specs += [pl.BlockSpec((1, 1, past, LANES), lambda b, h, qi: (b, li, 0, h)),
                     pl.BlockSpec((1, 1, past, LANES), lambda b, h, qi: (b, li, 0, h)),
                     pl.BlockSpec((tq, LANES), lambda b, h, qi: (qi, 0)),
                     pl.BlockSpec((tq, LANES), lambda b, h, qi: (qi, 0)),
                     pl.BlockSpec((L, LANES), lambda b, h, qi: (0, 0)),
                     pl.BlockSpec((L, LANES), lambda b, h, qi: (0, 0))]
        args += [ck, cv, cos_t, sin_t, cos_t, sin_t]
    in_specs.append(pl.BlockSpec((1, LANES), lambda b, h, qi: (0, 0)))
    args.append(da_norm_g[li:li + 1])
    lam_init = 0.8 - 0.6 * math.exp(-0.3 * layer)
    return pl.pallas_call(
        functools.partial(_attn_body, li=lam_init, cached=cached, tq=tq),
        grid=(nseq, heads, nq),
        in_specs=in_specs,
        out_specs=pl.BlockSpec((tq, LANES),
                               lambda b, h, qi: (row0 // tq + b * nq + qi, hy_ch // LANES + h)),
        out_shape=_sds((t, d), BF16),
        input_output_aliases={0: 0},
        compiler_params=_cp(("arbitrary", "arbitrary", "arbitrary"), big=True),
        name="diff_attention",
    )(*args)


def _dwconv_body(prev_ref, cur_ref, next_ref, w_ref, b_ref, lg_ref, lb_ref, o_ref, buf, acc,
                 *, rt, taps, dims):
    t_ctx, l_ctx, l_lat = dims
    i = pl.program_id(0)
    row0 = i * rt
    in_ctx = row0 < t_ctx
    seq_pos = jnp.where(in_ctx, row0 % l_ctx, (row0 - t_ctx) % l_lat)
    seq_len = jnp.where(in_ctx, l_ctx, l_lat)
    is_first = seq_pos == 0
    is_last = seq_pos + rt == seq_len
    buf[0:HALO, :] = jnp.where(is_first, 0.0, prev_ref[...])
    buf[HALO:HALO + rt, :] = cur_ref[...]
    buf[HALO + rt:2 * HALO + rt, :] = jnp.where(is_last, 0.0, next_ref[...])
    pad = taps // 2
    d = cur_ref.shape[1]
    rc = min(64, rt)

    def lane_chunk(c, carry):
        c0 = pl.multiple_of(c * LANES, LANES)
        w = w_ref[0, :, pl.ds(c0, LANES)]
        for r in range(rt // rc):
            a = jnp.zeros((rc, LANES), F32)
            for kk in range(taps):
                a = a + w[kk:kk + 1] * buf[pl.ds(r * rc + HALO - pad + kk, rc), pl.ds(c0, LANES)]
            acc[pl.ds(r * rc, rc), pl.ds(c0, LANES)] = a
        return carry

    lax.fori_loop(0, d // LANES, lane_chunk, 0)
    u = acc[...] + b_ref[0]
    mu = jnp.mean(u, axis=-1, keepdims=True)
    uc = u - mu
    var = jnp.mean(uc * uc, axis=-1, keepdims=True)
    y = uc * lax.rsqrt(var + EPS) * lg_ref[0] + lb_ref[0]
    o_ref[...] = (y * jax.nn.sigmoid(y)).astype(o_ref.dtype)


def _conformer_dwconv(u, li, cv_dw, cv_dwb, cv_ln_g, cv_ln_b, dims, rt):
    t, d = u.shape
    taps = cv_dw.shape[1]
    hb = rt // HALO
    nhb = t // HALO
    vec = lambda a: a.reshape(a.shape[0], 1, d)
    vspec = pl.BlockSpec((1, 1, d), lambda i: (li, 0, 0))
    return pl.pallas_call(
        functools.partial(_dwconv_body, rt=rt, taps=taps, dims=dims),
        grid=(t // rt,),
        in_specs=[pl.BlockSpec((HALO, d), lambda i: (jnp.maximum(i * hb - 1, 0), 0)),
                  pl.BlockSpec((rt, d), lambda i: (i, 0)),
                  pl.BlockSpec((HALO, d), lambda i: (jnp.minimum((i + 1) * hb, nhb - 1), 0)),
                  pl.BlockSpec((1, taps, d), lambda i: (li, 0, 0)),
                  vspec, vspec, vspec],
        out_specs=pl.BlockSpec((rt, d), lambda i: (i, 0)),
        out_shape=_sds((t, d), BF16),
        scratch_shapes=[pltpu.VMEM((rt + 2 * HALO, d), F32), pltpu.VMEM((rt, d), F32)],
        compiler_params=_cp(("arbitrary",)),
        name="conformer_dwconv",
    )(u, u, u, cv_dw, vec(cv_dwb), vec(cv_ln_g), vec(cv_ln_b))


def _scatter_body(pos_ref, pad_ref, h_ref, xs_ref, zrow, sem, zsem, *, rt, n_exp):
    i = pl.program_id(0)

    def issue(r, carry):
        for kk in range(TOP_K):
            p = pos_ref[(i * rt + r) * TOP_K + kk]
            pltpu.make_async_copy(h_ref.at[pl.ds(r, 1)], xs_ref.at[pl.ds(p, 1)], sem).start()
        return carry

    lax.fori_loop(0, rt, issue, 0)

    @pl.when(i == 0)
    def _():
        zrow[...] = jnp.zeros(zrow.shape, zrow.dtype)
        for e in range(n_exp):
            lo, hi = pad_ref[e], pad_ref[n_exp + e]

            def zissue(r, carry):
                pltpu.make_async_copy(zrow, xs_ref.at[pl.ds(r, 1)], zsem).start()
                return carry

            def zwait(r, carry):
                pltpu.make_async_copy(zrow, xs_ref.at[pl.ds(r, 1)], zsem).wait()
                return carry

            lax.fori_loop(lo, hi, zissue, 0)
            lax.fori_loop(lo, hi, zwait, 0)

    def drain(r, carry):
        for kk in range(TOP_K):
            pltpu.make_async_copy(h_ref.at[pl.ds(0, 1)], xs_ref.at[pl.ds(0, 1)], sem).wait()
        return carry

    lax.fori_loop(0, rt, drain, 0)


def _moe_scatter(h, pos, pad_ranges, n_rows, n_exp, rt):
    t, d = h.shape
    grid_spec = pltpu.PrefetchScalarGridSpec(
        num_scalar_prefetch=2, grid=(t // rt,),
        in_specs=[pl.BlockSpec((rt, d), lambda i, p, z: (i, 0))],
        out_specs=pl.BlockSpec(memory_space=pl.ANY),
        scratch_shapes=[pltpu.VMEM((1, d), h.dtype), pltpu.SemaphoreType.DMA,
                        pltpu.SemaphoreType.DMA])
    return pl.pallas_call(
        functools.partial(_scatter_body, rt=rt, n_exp=n_exp),
        grid_spec=grid_spec, out_shape=_sds((n_rows, d), h.dtype),
        compiler_params=_cp(("arbitrary",)),
        name="moe_scatter",
    )(pos, pad_ranges, h)


def _combine_body(pos_ref, x_ref, meta_ref, gate_ref, ys_ref, o_ref, buf, sem, *, rt):
    i = pl.program_id(0)

    def issue(r, carry):
        for kk in range(TOP_K):
            p = pos_ref[(i * rt + r) * TOP_K + kk]
            pltpu.make_async_copy(ys_ref.at[pl.ds(p, 1)], buf.at[kk, pl.ds(r, 1)], sem).start()
        return carry

    lax.fori_loop(0, rt, issue, 0)

    def drain(r, carry):
        for kk in range(TOP_K):
            pltpu.make_async_copy(ys_ref.at[pl.ds(0, 1)], buf.at[kk, pl.ds(0, 1)], sem).wait()
        return carry

    lax.fori_loop(0, rt, drain, 0)
    meta = meta_ref[...]
    f = meta[:, 2:3] * buf[0] + meta[:, 3:4] * buf[1]
    o_ref[...] = x_ref[...] + gate_ref[0] * f


def _moe_combine(x, ys, pos, meta, modr, layer, chunk, dims, rt):
    t, d = x.shape
    t_ctx, l_lat = dims
    grid_spec = pltpu.PrefetchScalarGridSpec(
        num_scalar_prefetch=1, grid=(t // rt,),
        in_specs=[pl.BlockSpec((rt, d), lambda i, p: (i, 0)),
                  pl.BlockSpec((rt, LANES), lambda i, p: (i, 0)),
                  pl.BlockSpec((1, 1, d), lambda i, p: (
                      layer * MOD_ROWS + _mod_row(i * rt, t_ctx, l_lat), 0, chunk)),
                  pl.BlockSpec(memory_space=pl.ANY)],
        out_specs=pl.BlockSpec((rt, d), lambda i, p: (i, 0)),
        scratch_shapes=[pltpu.VMEM((TOP_K, rt, d), F32), pltpu.SemaphoreType.DMA])
    return pl.pallas_call(
        functools.partial(_combine_body, rt=rt),
        grid_spec=grid_spec, out_shape=_sds((t, d), F32),
        compiler_params=_cp(("arbitrary",)),
        name="moe_combine",
    )(pos, x, meta, modr, ys)


def _routing_tables(meta, n_exp, tm, n_tiles):
    ids = meta[:, 0:TOP_K].astype(jnp.int32).reshape(-1)
    onehot = (ids[:, None] == jnp.arange(n_exp)[None, :]).astype(jnp.int32)
    csum = jnp.cumsum(onehot, axis=0)
    counts = csum[-1]
    rank = jnp.sum(onehot * csum, axis=1) - 1
    padded = (counts + tm - 1) // tm * tm
    ends = jnp.cumsum(padded)
    starts = ends - padded
    pos = (starts[ids] + rank).astype(jnp.int32)
    n_active = (ends[-1] // tm).astype(jnp.int32)
    tile_start = jnp.arange(n_tiles, dtype=jnp.int32) * tm
    te = jnp.sum((tile_start[:, None] >= ends[None, :]).astype(jnp.int32), axis=1)
    te_last = jnp.sum((jnp.maximum(ends[-1] - tm, 0) >= ends).astype(jnp.int32))
    te = jnp.where(tile_start < ends[-1], te, te_last).astype(jnp.int32)
    hi = ends.at[-1].set(n_tiles * tm)
    pad_ranges = jnp.concatenate([starts + counts, hi]).astype(jnp.int32)
    return pos, te, n_active.reshape(1), pad_ranges


def kernel(x_prompt, x_sample, cache_k, cache_v, c, c_ctx, norm_mix_g, norm_ffn_g, w_mod, b_mod,
           w_in, w_out, hy_short_w, hy_short_b, hy_w1, hy_b1, hy_fr1, hy_w2, hy_b2, hy_fr2, hy_w3,
           hy_fbias, da_lq1, da_lk1, da_lq2, da_lk2, da_norm_g, cv_w1, cv_b1, cv_dw, cv_dwb,
           cv_ln_g, cv_ln_b, cv_w2, cv_b2, ff_w1, ff_w3, ff_w2, moe_router, moe_w1, moe_w3, moe_w2,
           final_g):
    b_ctx, l_ctx, d = x_prompt.shape
    b_lat, l_lat, _ = x_sample.shape
    t_ctx, t_lat = b_ctx * l_ctx, b_lat * l_lat
    t = t_ctx + t_lat
    depth = w_mod.shape[0]
    hy_ch = hy_fbias.shape[-1]
    da_width = d - hy_ch
    heads = da_width // DA_VD
    n_exp = moe_router.shape[-1]
    d_ff = ff_w1.shape[-1]
    e_ff = moe_w1.shape[-1]
    past = cache_k.shape[2]
    n_even = w_in.shape[0]
    dims = (t_ctx, l_lat)
    rt = min(256, l_ctx)
    tm_big = min(1024, l_lat, t_ctx)
    tm_res = min(512, l_lat, t_ctx)
    tm_moe = min(256, l_ctx)
    assert b_lat + 1 <= MOD_ROWS and t_ctx % tm_big == 0 and l_lat % tm_big == 0

    x = jnp.concatenate([x_prompt.reshape(t_ctx, d), x_sample.reshape(t_lat, d)], axis=0)
    cond = jnp.zeros((MOD_ROWS, d), F32).at[0].set(c_ctx).at[1:1 + b_lat].set(c)
    modr = _mods(cond, w_mod, b_mod).reshape(depth * MOD_ROWS, 1, 6 * d)

    tables = {}
    for L in sorted({l_ctx, l_lat}):
        f_tab, g_tab = _dft_tables(L)
        tables[L] = (f_tab, f_tab.astype(BF16), g_tab.astype(BF16))
    cos_t, sin_t = _rope_tables(l_lat)
    ck = cache_k.reshape(b_lat, n_even, past, heads * DA_VD)
    cv = cache_v.reshape(b_lat, n_even, past, heads * DA_VD)
    lams = (da_lq1, da_lk1, da_lq2, da_lk2)
    router_pad = jnp.pad(moe_router, ((0, 0), (0, 0), (0, LANES - n_exp)))
    moe_w1r = moe_w1.reshape(-1, d, e_ff)
    moe_w3r = moe_w3.reshape(-1, d, e_ff)
    moe_w2r = moe_w2.reshape(-1, e_ff, d)
    n_tiles = (t * TOP_K) // tm_moe + n_exp
    n_rows = n_tiles * tm_moe
    vec3 = lambda a: a.reshape(a.shape[0], 1, a.shape[-1])

    ks, vs = [], []
    for layer in range(depth):
        li = layer // 2
        if layer % 2 == 0:
            h = _norm_mod(x, norm_mix_g, layer, modr, 0, 1, dims, rt)
            proj = _matmul(h, [w_in], li, [0], w_in.shape[-1], tm=tm_big,
                           tn=min(1024, w_in.shape[-1]), name="in_proj")
            ymix = jnp.zeros((t, d), BF16)
            for (nseq, L, row0) in ((b_ctx, l_ctx, 0), (b_lat, l_lat, t_ctx)):
                f_tab, f_bf, g_bf = tables[L]
                hs = _hyena_filters(L, li, hy_w1, hy_b1, hy_fr1, hy_w2, hy_b2, hy_fr2, hy_w3,
                                    f_tab, hy_ch)
                ymix = _hyena(proj, ymix, li, nseq, L, row0 // L, hy_short_w, hy_short_b,
                              hy_fbias, hs, f_bf, g_bf, hy_ch)
            ymix = _attention(proj, ymix, li, layer, b_ctx, l_ctx, 0, lams, da_norm_g, hy_ch,
                              da_width)
            ymix = _attention(proj, ymix, li, layer, b_lat, l_lat, t_ctx, lams, da_norm_g, hy_ch,
                              da_width, cache=(ck, cv, cos_t, sin_t))
            kc0 = 3 * hy_ch + da_width
            ks.append(proj[:t_ctx, kc0:kc0 + da_width].reshape(b_ctx, l_ctx, heads, DA_VD))
            vs.append(proj[:t_ctx, kc0 + da_width:kc0 + 2 * da_width]
                      .reshape(b_ctx, l_ctx, heads, DA_VD))
            x = _matmul(ymix, [w_out], li, [0], d, tm=tm_res, tn=min(1024, d), epi="plain",
                        res=x, gate=(modr, layer, 2, dims), name="out_proj")
            h = _norm_mod(x, norm_ffn_g, layer, modr, 3, 4, dims, rt)
            hid = _matmul(h, [ff_w1, ff_w3], li, [0, 0], d_ff, tm=tm_big, tn=min(512, d_ff),
                          epi="swiglu", out_dtype=BF16, name="ffn_up")
            x = _matmul(hid, [ff_w2], li, [0], d, tm=tm_res, tn=min(512, d), res=x,
                        gate=(modr, layer, 5, dims), name="ffn_down")
        else:
            h = _norm_mod(x, norm_mix_g, layer, modr, 0, 1, dims, rt)
            u = _matmul(h, [cv_w1, cv_w1], li, [0, d], d, tm=tm_big, tn=min(512, d), epi="glu",
                        biases=[vec3(cv_b1), vec3(cv_b1)], bidx=li, name="conformer_up")
            u = _conformer_dwconv(u, li, cv_dw, cv_dwb, cv_ln_g, cv_ln_b, (t_ctx, l_ctx, l_lat),
                                  rt)
            x = _matmul(u, [cv_w2], li, [0], d, tm=tm_res, tn=min(1024, d), biases=[vec3(cv_b2)],
                        bidx=li, res=x, gate=(modr, layer, 2, dims), name="conformer_down")
            h32, meta = _norm_router(x, norm_ffn_g, layer, modr, 3, 4, router_pad[li], n_exp,
                                     dims, rt)
            pos, te, n_act, pad_ranges = _routing_tables(meta, n_exp, tm_moe, n_tiles)
            xs = _moe_scatter(h32, pos, pad_ranges, n_rows, n_exp, rt)
            group = (te, n_act, li * n_exp)
            hid = _matmul(xs, [moe_w1r, moe_w3r], 0, [0, 0], e_ff, tm=tm_moe,
                          tn=min(1024, e_ff), epi="swiglu", out_dtype=BF16, group=group,
                          name="moe_up")
            ys = _matmul(hid, [moe_w2r], 0, [0], d, tm=tm_moe, tn=min(512, d), group=group,
                         name="moe_down")
            x = _moe_combine(x, ys, pos, meta, modr, layer, 5, dims, rt)

    y = _final_norm(x, final_g, rt)
    y_prompt = y[:t_ctx].reshape(b_ctx, l_ctx, d)
    y_sample = y[t_ctx:].reshape(b_lat, l_lat, d)
    return (y_prompt, y_sample, jnp.stack(ks, axis=1), jnp.stack(vs, axis=1))
```

```python
import functools
import math

import jax
import jax.numpy as jnp
from jax import lax
from jax.experimental import pallas as pl
from jax.experimental.pallas import tpu as pltpu

F32 = jnp.float32
BF16 = jnp.bfloat16
HIGHEST = lax.Precision.HIGHEST

EPS = 1e-6
GRID_W = 64
HY_EMB = 33
HY_FILT_HID = 64
HY_FAST_PCT = 0.3
HY_SLOW_PCT = 1.5
HY_TARGET = 1e-2
DA_DH = 64
DA_VD = 2 * DA_DH
ROPE_THETA = 10000.0
TOP_K = 2
MOD_ROWS = 16
LANES = 128
V7X_VMEM_LIMIT = 56 * 2**20
HALO = 16


def _cp(sem, big=False):
    return pltpu.CompilerParams(dimension_semantics=sem,
                                vmem_limit_bytes=V7X_VMEM_LIMIT if big else None)


def _sds(shape, dtype):
    return jax.ShapeDtypeStruct(shape, dtype)


def _fit(n, pref):
    best = LANES
    for w in range(LANES, min(n, pref) + 1, LANES):
        if n % w == 0:
            best = w
    return best


def _mod_row(row0, t_ctx, l_lat):
    return jnp.where(row0 < t_ctx, 0, 1 + (row0 - t_ctx) // l_lat)


def _mod_body(c_ref, w_ref, b_ref, o_ref):
    c = c_ref[...]
    s = (c * jax.nn.sigmoid(c)).astype(BF16)
    o_ref[0] = jnp.dot(s, w_ref[0].astype(BF16), preferred_element_type=F32) + b_ref[0]


def _mods(cond, w_mod, b_mod):
    depth, d, nm = w_mod.shape
    tn = _fit(nm, 1024)
    return pl.pallas_call(
        _mod_body,
        grid=(depth, nm // tn),
        in_specs=[pl.BlockSpec((MOD_ROWS, d), lambda l, n: (0, 0)),
                  pl.BlockSpec((1, d, tn), lambda l, n: (l, 0, n)),
                  pl.BlockSpec((1, 1, tn), lambda l, n: (l, 0, n))],
        out_specs=pl.BlockSpec((1, MOD_ROWS, tn), lambda l, n: (l, 0, n)),
        out_shape=_sds((depth, MOD_ROWS, nm), F32),
        compiler_params=_cp(("arbitrary", "arbitrary"), big=True),
        name="modulation",
    )(cond, w_mod, b_mod.reshape(depth, 1, nm))


def _rms(x, g):
    return x * lax.rsqrt(jnp.mean(x * x, axis=-1, keepdims=True) + EPS) * g


def _norm_mod_body(x_ref, g_ref, sh_ref, sc_ref, o_ref):
    y = _rms(x_ref[...], g_ref[0])
    o_ref[...] = (y * (1.0 + sc_ref[0]) + sh_ref[0]).astype(o_ref.dtype)


def _pack_halves(h):
    half = h.shape[1] // 2
    bits = lax.bitcast_convert_type(h.astype(BF16).astype(F32), jnp.uint32)
    return (bits[:, :half] >> 16) | (bits[:, half:] & jnp.uint32(0xFFFF0000))


def _unpack_halves(words):
    lo = lax.bitcast_convert_type(words << 16, F32).astype(BF16)
    hi = lax.bitcast_convert_type(words & jnp.uint32(0xFFFF0000), F32).astype(BF16)
    return lo, hi


def _norm_router_body(x_ref, g_ref, sh_ref, sc_ref, r_ref, h_ref, meta_ref, *, n_exp):
    y = _rms(x_ref[...], g_ref[0])
    h = y * (1.0 + sc_ref[0]) + sh_ref[0]
    h_ref[...] = _pack_halves(h)
    logits = jnp.dot(h, r_ref[...], precision=HIGHEST, preferred_element_type=F32)
    lane = lax.broadcasted_iota(jnp.int32, logits.shape, 1)
    neg = jnp.float32(-jnp.inf)
    logits = jnp.where(lane < n_exp, logits, neg)
    m1 = jnp.max(logits, axis=-1, keepdims=True)
    i1 = jnp.min(jnp.where(logits == m1, lane, LANES), axis=-1, keepdims=True)
    rest = jnp.where(lane == i1, neg, logits)
    m2 = jnp.max(rest, axis=-1, keepdims=True)
    i2 = jnp.min(jnp.where(rest == m2, lane, LANES), axis=-1, keepdims=True)
    e2 = jnp.exp(m2 - m1)
    g1 = 1.0 / (1.0 + e2)
    g2 = e2 * g1
    meta = jnp.where(lane == 0, i1.astype(F32),
                     jnp.where(lane == 1, i2.astype(F32),
                               jnp.where(lane == 2, g1, jnp.where(lane == 3, g2, 0.0))))
    meta_ref[...] = meta


def _final_norm_body(x_ref, g_ref, o_ref):
    o_ref[...] = _rms(x_ref[...], g_ref[...])


def _mod_spec(rt, d, layer, chunk, dims):
    t_ctx, l_lat = dims
    return pl.BlockSpec(
        (1, 1, d), lambda i: (layer * MOD_ROWS + _mod_row(i * rt, t_ctx, l_lat), 0, chunk))


def _norm_mod(x, g_all, layer, modr, sh_chunk, sc_chunk, dims, rt):
    t, d = x.shape
    return pl.pallas_call(
        _norm_mod_body,
        grid=(t // rt,),
        in_specs=[pl.BlockSpec((rt, d), lambda i: (i, 0)),
                  pl.BlockSpec((1, 1, d), lambda i: (layer, 0, 0)),
                  _mod_spec(rt, d, layer, sh_chunk, dims),
                  _mod_spec(rt, d, layer, sc_chunk, dims)],
        out_specs=pl.BlockSpec((rt, d), lambda i: (i, 0)),
        out_shape=_sds((t, d), BF16),
        compiler_params=_cp(("arbitrary",)),
        name="norm_modulate",
    )(x, g_all.reshape(-1, 1, d), modr, modr)


def _norm_router(x, g_all, layer, modr, sh_chunk, sc_chunk, router_pad, n_exp, dims, rt):
    t, d = x.shape
    return pl.pallas_call(
        functools.partial(_norm_router_body, n_exp=n_exp),
        grid=(t // rt,),
        in_specs=[pl.BlockSpec((rt, d), lambda i: (i, 0)),
                  pl.BlockSpec((1, 1, d), lambda i: (layer, 0, 0)),
                  _mod_spec(rt, d, layer, sh_chunk, dims),
                  _mod_spec(rt, d, layer, sc_chunk, dims),
                  pl.BlockSpec((d, LANES), lambda i: (0, 0))],
        out_specs=[pl.BlockSpec((rt, d // 2), lambda i: (i, 0)),
                   pl.BlockSpec((rt, LANES), lambda i: (i, 0))],
        out_shape=[_sds((t, d // 2), jnp.uint32), _sds((t, LANES), F32)],
        compiler_params=_cp(("arbitrary",)),
        name="norm_router",
    )(x, g_all.reshape(-1, 1, d), modr, modr, router_pad)


def _final_norm(x, g, rt):
    t, d = x.shape
    return pl.pallas_call(
        _final_norm_body,
        grid=(t // rt,),
        in_specs=[pl.BlockSpec((rt, d), lambda i: (i, 0)),
                  pl.BlockSpec((1, d), lambda i: (0, 0))],
        out_specs=pl.BlockSpec((rt, d), lambda i: (i, 0)),
        out_shape=_sds((t, d), F32),
        compiler_params=_cp(("arbitrary",)),
        name="final_norm",
    )(x, g.reshape(1, d))


def _mm_body(*refs, nw, has_bias, epi, grouped, has_res):
    it = iter(refs)
    if grouped:
        te_ref = next(it)
        na_ref = next(it)
    x_ref = next(it)
    w_refs = [next(it) for _ in range(nw)]
    b_refs = [next(it) for _ in range(nw)] if has_bias else []
    if has_res:
        res_ref = next(it)
        gate_ref = next(it)
    o_ref = next(it)
    wc_refs = [next(it) for _ in range(nw)]

    i = pl.program_id(1)
    if grouped:
        recast = (i == 0) | (te_ref[i] != te_ref[jnp.maximum(i - 1, 0)])
    else:
        recast = i == 0

    @pl.when(recast)
    def _():
        for w_ref, wc in zip(w_refs, wc_refs):
            wc[...] = w_ref[0].astype(BF16)

    def compute():
        if x_ref.dtype == jnp.uint32:
            lo, hi = _unpack_halves(x_ref[...])
            half = lo.shape[1]
            accs = [jnp.dot(lo, wc[0:half, :], preferred_element_type=F32)
                    + jnp.dot(hi, wc[half:2 * half, :], preferred_element_type=F32)
                    for wc in wc_refs]
        else:
            x = x_ref[...].astype(BF16)
            accs = [jnp.dot(x, wc[...], preferred_element_type=F32) for wc in wc_refs]
        if has_bias:
            accs = [a + b[0] for a, b in zip(accs, b_refs)]
        if epi == "plain":
            y = accs[0]
        elif epi == "swiglu":
            y = accs[0] * jax.nn.sigmoid(accs[0]) * accs[1]
        elif epi == "glu":
            y = accs[0] * jax.nn.sigmoid(accs[1])
        if has_res:
            y = res_ref[...] + gate_ref[0] * y
        o_ref[...] = y.astype(o_ref.dtype)

    if grouped:
        pl.when(i < na_ref[0])(compute)

        @pl.when(i >= na_ref[0])
        def _():
            o_ref[...] = jnp.zeros(o_ref.shape, o_ref.dtype)
    else:
        compute()


def _matmul(x, ws, widx, col0s, n_out, *, tm, tn, epi="plain", biases=None, bidx=0,
            res=None, gate=None, out_dtype=F32, group=None, name="matmul"):
    m, kx = x.shape
    k = ws[0].shape[1]
    nw = len(ws)
    grouped = group is not None
    has_bias = biases is not None
    has_res = res is not None
    tn = _fit(n_out, tn)
    grid = (n_out // tn, m // tm)

    if grouped:
        te, na, base = group

        def wmap(c0):
            return lambda n, i, te_r, na_r: (base + te_r[i], 0, c0 // tn + n)

        def rmap(f):
            return lambda n, i, te_r, na_r: f(n, i)

        xmap = lambda n, i, te_r, na_r: (jnp.minimum(i, na_r[0] - 1), 0)
    else:
        def wmap(c0):
            return lambda n, i: (widx, 0, c0 // tn + n)

        def rmap(f):
            return f

        xmap = lambda n, i: (i, 0)

    in_specs = [pl.BlockSpec((tm, kx), xmap)]
    args = [x]
    for w, c0 in zip(ws, col0s):
        in_specs.append(pl.BlockSpec((1, k, tn), wmap(c0)))
        args.append(w)
    if has_bias:
        for b, c0 in zip(biases, col0s):
            in_specs.append(pl.BlockSpec((1, 1, tn), rmap(
                lambda n, i, c0=c0: (bidx, 0, c0 // tn + n))))
            args.append(b)
    if has_res:
        modr, layer, chunk, (t_ctx, l_lat) = gate
        in_specs.append(pl.BlockSpec((tm, tn), rmap(lambda n, i: (i, n))))
        args.append(res)
        in_specs.append(pl.BlockSpec((1, 1, tn), rmap(
            lambda n, i: (layer * MOD_ROWS + _mod_row(i * tm, t_ctx, l_lat), 0,
                          chunk * (n_out // tn) + n))))
        args.append(modr)
    out_spec = pl.BlockSpec((tm, tn), rmap(lambda n, i: (i, n)))
    scratch = [pltpu.VMEM((k, tn), BF16) for _ in range(nw)]
    body = functools.partial(_mm_body, nw=nw, has_bias=has_bias, epi=epi, grouped=grouped,
                             has_res=has_res)
    if grouped:
        grid_spec = pltpu.PrefetchScalarGridSpec(
            num_scalar_prefetch=2, grid=grid, in_specs=in_specs, out_specs=out_spec,
            scratch_shapes=scratch)
        return pl.pallas_call(body, grid_spec=grid_spec, out_shape=_sds((m, n_out), out_dtype),
                              compiler_params=_cp(("arbitrary", "arbitrary"), big=True),
                              name=name)(te, na, *args)
    return pl.pallas_call(body, grid=grid, in_specs=in_specs, out_specs=out_spec,
                          out_shape=_sds((m, n_out), out_dtype), scratch_shapes=scratch,
                          compiler_params=_cp(("arbitrary", "arbitrary"), big=True),
                          name=name)(*args)


def _dft_basis(r, s, L):
    n = 2 * L
    f = jnp.where(r < L, r, r - L)
    k = (f * s + jnp.where(r < L, 0, n // 4)) & (n - 1)
    val = jnp.cos(k.astype(F32) * (2.0 * math.pi / n))
    nyq = (1 - 2 * (s & 1)).astype(F32)
    return jnp.where(r == L, nyq, val)


def _dft_body(f_ref, g_ref, *, L, rb):
    i = pl.program_id(0)
    r = i * rb + lax.broadcasted_iota(jnp.int32, (rb, L), 0)
    s = lax.broadcasted_iota(jnp.int32, (rb, L), 1)
    f_ref[...] = _dft_basis(r, s, L)
    t = lax.broadcasted_iota(jnp.int32, (L, rb), 0)
    r2 = i * rb + lax.broadcasted_iota(jnp.int32, (L, rb), 1)
    scale = jnp.where((r2 == 0) | (r2 == L), 1.0 / (2 * L), 1.0 / L)
    g_ref[...] = _dft_basis(r2, t, L) * scale


def _dft_tables(L):
    rb = min(256, L)
    return pl.pallas_call(
        functools.partial(_dft_body, L=L, rb=rb),
        grid=(2 * L // rb,),
        out_specs=[pl.BlockSpec((rb, L), lambda i: (i, 0)),
                   pl.BlockSpec((L, rb), lambda i: (0, i))],
        out_shape=[_sds((2 * L, L), F32), _sds((L, 2 * L), F32)],
        compiler_params=_cp(("arbitrary",)),
        name="dft_tables",
    )()


def _filter_body(z_ref, w1_ref, b1_ref, fr1_ref, w2_ref, b2_ref, fr2_ref, w3a_ref, w3b_ref,
                 dl_ref, f_ref, o_ref, *, L):
    z = z_ref[...]
    dot = functools.partial(jnp.dot, precision=HIGHEST, preferred_element_type=F32)
    h = jnp.sin(fr1_ref[...] * (dot(z, w1_ref[0]) + b1_ref[...]))
    h = jnp.sin(fr2_ref[...] * (dot(h, w2_ref[0]) + b2_ref[...]))
    t = z[:, 0:1]
    decay = jnp.exp(-t * dl_ref[...])
    hf = dot(h, w3a_ref[0]) * decay
    hb = dot(h, w3b_ref[0]) * decay
    row = lax.broadcasted_iota(jnp.int32, (L, 1), 0)
    hb = jnp.where(row == 0, 0.0, hb)
    nrm = (jnp.sum(jnp.abs(hf), axis=0, keepdims=True)
           + jnp.sum(jnp.abs(hb), axis=0, keepdims=True))
    p = dot(f_ref[0:L, :], hf + hb)
    q = dot(f_ref[L:2 * L, :], hf - hb)
    sign = (1 - 2 * (row & 1)).astype(F32)
    nyq_fix = 2.0 * jnp.sum(sign * hb, axis=0, keepdims=True)
    q = q + jnp.where(row == 0, nyq_fix, 0.0)
    inv = 1.0 / nrm
    o_ref[0, 0:L, :] = p * inv
    o_ref[0, L:2 * L, :] = q * inv


def _pad2(a, rows, cols):
    return jnp.pad(a, ((0, rows - a.shape[0]), (0, cols - a.shape[1])))


def _hyena_filters(L, li, hy_w1, hy_b1, hy_fr1, hy_w2, hy_b2, hy_fr2, hy_w3, f_tab, hy_ch):
    bands = (HY_EMB - 1) // 2
    pos = jnp.arange(L, dtype=F32)
    t = jnp.linspace(0.0, 1.0, L, dtype=F32)
    f = jnp.linspace(1e-4, bands - 1, bands, dtype=F32)
    ang = 2.0 * math.pi * pos[:, None] * f[None, :] / L
    z = jnp.concatenate([t[:, None], jnp.cos(ang), -jnp.sin(ang)], axis=-1)
    z = _pad2(z, L, LANES)
    max_decay = math.log(HY_TARGET) / HY_FAST_PCT
    min_decay = math.log(HY_TARGET) / HY_SLOW_PCT
    deltas = jnp.abs(jnp.linspace(min_decay, max_decay, hy_ch, dtype=F32)).reshape(1, hy_ch)
    n_even = hy_w1.shape[0]
    w1 = jnp.pad(hy_w1, ((0, 0), (0, LANES - HY_EMB), (0, LANES - HY_FILT_HID)))
    w2 = jnp.pad(hy_w2, ((0, 0), (0, LANES - HY_FILT_HID), (0, LANES - HY_FILT_HID)))
    w3 = jnp.pad(hy_w3, ((0, 0), (0, LANES - HY_FILT_HID), (0, 0)))
    padv = lambda a: _pad2(a[li:li + 1], 1, LANES)
    cb = min(256, hy_ch)
    ncb = hy_ch // cb
    vec = pl.BlockSpec((1, LANES), lambda j: (0, 0))
    sq = pl.BlockSpec((1, LANES, LANES), lambda j: (li, 0, 0))
    return pl.pallas_call(
        functools.partial(_filter_body, L=L),
        grid=(2 * ncb,),
        in_specs=[pl.BlockSpec((L, LANES), lambda j: (0, 0)),
                  sq, vec, vec, sq, vec, vec,
                  pl.BlockSpec((1, LANES, cb), lambda j: (li, 0, j)),
                  pl.BlockSpec((1, LANES, cb), lambda j: (li, 0, 2 * ncb + j)),
                  pl.BlockSpec((1, cb), lambda j: (0, j % ncb)),
                  pl.BlockSpec((2 * L, L), lambda j: (0, 0))],
        out_specs=pl.BlockSpec((1, 2 * L, cb), lambda j: (j // ncb, 0, j % ncb)),
        out_shape=_sds((2, 2 * L, hy_ch), F32),
        compiler_params=_cp(("arbitrary",), big=True),
        name="hyena_filters",
    )(z, w1, padv(hy_b1), padv(hy_fr1), w2, padv(hy_b2), padv(hy_fr2), w3, w3, deltas, f_tab)


def _hyena_body(ymix_ref, v_ref, x1_ref, x2_ref, swv_ref, sw1_ref, sw2_ref, sbv_ref, sb1_ref,
                sb2_ref, fb_ref, hs_ref, f_ref, g_ref, o_ref, *, L):
    del ymix_ref
    row = lax.broadcasted_iota(jnp.int32, (L, 1), 0)

    def short(x_ref, w_ref, b_ref):
        x = x_ref[...]
        w = w_ref[0]
        xm = jnp.where(row == 0, 0.0, pltpu.roll(x, 1, 0))
        xp = jnp.where(row == L - 1, 0.0, pltpu.roll(x, L - 1, 0))
        return xm * w[0:1] + x * w[1:2] + xp * w[2:3] + b_ref[0]

    def fconv(z, order):
        zf = jnp.dot(f_ref[...], z.astype(BF16), preferred_element_type=F32)
        p, q = zf[0:L], zf[L:2 * L]
        hp, hq = hs_ref[order, 0:L, :], hs_ref[order, L:2 * L, :]
        yr = jnp.where(row == 0, p * hp, p * hp - q * hq)
        yq = jnp.where(row == 0, q * hq, p * hq + q * hp)
        return (jnp.dot(g_ref[:, 0:L], yr.astype(BF16), preferred_element_type=F32)
                + jnp.dot(g_ref[:, L:2 * L], yq.astype(BF16), preferred_element_type=F32))

    v = short(v_ref, swv_ref, sbv_ref)
    x1 = short(x1_ref, sw1_ref, sb1_ref)
    x2 = short(x2_ref, sw2_ref, sb2_ref)
    fb = fb_ref[0]
    z1 = x1 * (fconv(v, 0) + fb[0:1] * v)
    o_ref[...] = (x2 * (fconv(z1, 1) + fb[1:2] * z1)).astype(o_ref.dtype)


def _hyena(proj, ymix, li, nseq, L, row_blk0, hy_short_w, hy_short_b, hy_fbias, hs, f_bf, g_bf,
           hy_ch):
    t, d = ymix.shape
    cb = min(256, hy_ch)
    ncb = hy_ch // cb
    rowmap = lambda c: (lambda j, b: (row_blk0 + b, c * ncb + j))
    wmap = lambda c: (lambda j, b: (li, 0, c * ncb + j))
    sb = hy_short_b.reshape(hy_short_b.shape[0], 1, -1)
    in_specs = [pl.BlockSpec(memory_space=pl.ANY),
                pl.BlockSpec((L, cb), rowmap(0)), pl.BlockSpec((L, cb), rowmap(1)),
                pl.BlockSpec((L, cb), rowmap(2)),
                pl.BlockSpec((1, 3, cb), wmap(0)), pl.BlockSpec((1, 3, cb), wmap(1)),
                pl.BlockSpec((1, 3, cb), wmap(2)),
                pl.BlockSpec((1, 1, cb), wmap(0)), pl.BlockSpec((1, 1, cb), wmap(1)),
                pl.BlockSpec((1, 1, cb), wmap(2)),
                pl.BlockSpec((1, 2, cb), lambda j, b: (li, 0, j)),
                pl.BlockSpec((2, 2 * L, cb), lambda j, b: (0, 0, j)),
                pl.BlockSpec((2 * L, L), lambda j, b: (0, 0)),
                pl.BlockSpec((L, 2 * L), lambda j, b: (0, 0))]
    args = [ymix, proj, proj, proj, hy_short_w, hy_short_w, hy_short_w, sb, sb, sb, hy_fbias, hs,
            f_bf, g_bf]
    return pl.pallas_call(
        functools.partial(_hyena_body, L=L),
        grid=(ncb, nseq),
        in_specs=in_specs,
        out_specs=pl.BlockSpec((L, cb), lambda j, b: (row_blk0 + b, j)),
        out_shape=_sds((t, d), BF16),
        input_output_aliases={0: 0},
        compiler_params=_cp(("arbitrary", "arbitrary"), big=True),
        name="hyena_mixer",
    )(*args)


def _rope_tables(L):
    half = DA_DH // 2
    nf = half // 2
    lane = jnp.arange(LANES)
    sub = lane % DA_DH
    idx = sub % half
    inv = ROPE_THETA ** (-(idx % nf).astype(F32) / nf)
    pos = jnp.arange(L)
    p = jnp.where((sub // half)[None, :] == 0, (pos // GRID_W)[:, None], (pos % GRID_W)[:, None])
    ang = p.astype(F32) * inv[None, :]
    sgn = jnp.where(idx < nf, -1.0, 1.0)[None, :]
    return jnp.cos(ang), jnp.sin(ang) * sgn


def _attn_body(*refs, li, cached, tq):
    it = iter(refs)
    next(it)
    lq1, lk1, lq2, lk2 = [next(it) for _ in range(4)]
    q_ref, k_ref, v_ref = next(it), next(it), next(it)
    if cached:
        ck_ref, cv_ref, cq_ref, sq_ref, ckk_ref, skk_ref = [next(it) for _ in range(6)]
    g_ref = next(it)
    o_ref = next(it)

    lam = (jnp.exp(jnp.sum(lq1[...] * lk1[...], keepdims=True))
           - jnp.exp(jnp.sum(lq2[...] * lk2[...], keepdims=True)) + li)
    lane = lax.broadcasted_iota(jnp.int32, (1, LANES), 1)
    first = lane < DA_DH
    nf = DA_DH // 4

    def rope(x, c_ref, s_ref):
        swap = jnp.where((lane % (2 * nf)) < nf, pltpu.roll(x, LANES - nf, 1), pltpu.roll(x, nf, 1))
        return x * c_ref[...] + swap * s_ref[...]

    q = q_ref[...]
    k = k_ref[...]
    if cached:
        q = rope(q, cq_ref, sq_ref)
        k = rope(k, ckk_ref, skk_ref)
    q = q * (DA_DH ** -0.5)
    qs = [jnp.where(first, q, 0.0).astype(BF16), jnp.where(first, 0.0, q).astype(BF16)]
    nt = (((1,), (1,)), ((), ()))
    kb = k.astype(BF16)
    vb = v_ref[...].astype(BF16)
    if cached:
        ckb = ck_ref[0, 0].astype(BF16)
        cvb = cv_ref[0, 0].astype(BF16)

    def probs(qm):
        s = lax.dot_general(qm, kb, nt, preferred_element_type=F32)
        m = jnp.max(s, axis=-1, keepdims=True)
        if cached:
            sc = lax.dot_general(qm, ckb, nt, preferred_element_type=F32)
            m = jnp.maximum(m, jnp.max(sc, axis=-1, keepdims=True))
            ec = jnp.exp(sc - m)
        e = jnp.exp(s - m)
        den = jnp.sum(e, axis=-1, keepdims=True)
        if cached:
            den = den + jnp.sum(ec, axis=-1, keepdims=True)
            return e / den, ec / den
        return e / den, None

    p1, pc1 = probs(qs[0])
    p2, pc2 = probs(qs[1])
    o = jnp.dot((p1 - lam * p2).astype(BF16), vb, preferred_element_type=F32)
    if cached:
        o = o + jnp.dot((pc1 - lam * pc2).astype(BF16), cvb, preferred_element_type=F32)
    o = _rms(o, g_ref[0]) * (1.0 - li)
    o_ref[...] = o.astype(o_ref.dtype)


def _attention(proj, ymix, li, layer, nseq, L, row0, lams, da_norm_g, hy_ch, da_width,
               cache=None):
    t, d = ymix.shape
    heads = da_width // DA_VD
    cached = cache is not None
    tq = min(512, L)
    nq = L // tq
    qc0 = 3 * hy_ch // LANES
    kc0 = qc0 + da_width // LANES
    vc0 = kc0 + da_width // LANES
    lam_spec = pl.BlockSpec((1, DA_DH), lambda b, h, qi: (0, 0))
    in_specs = [pl.BlockSpec(memory_space=pl.ANY)] + [lam_spec] * 4 + [
        pl.BlockSpec((tq, LANES), lambda b, h, qi: (row0 // tq + b * nq + qi, qc0 + h)),
        pl.BlockSpec((L, LANES), lambda b, h, qi: (row0 // L + b, kc0 + h)),
        pl.BlockSpec((L, LANES), lambda b, h, qi: (row0 // L + b, vc0 + h))]
    args = [ymix] + [a[li:li + 1] for a in lams] + [proj, proj, proj]
    if cached:
        ck, cv, cos_t, sin_t = cache
        past = ck.shape[2]
        in_specs += [pl.BlockSpec((1, 1, past, LANES), lambda b, h, qi: (b, li, 0, h)),
                     pl.BlockSpec((1, 1, past, LANES), lambda b, h, qi: (b, li, 0, h)),
                     pl.BlockSpec((tq, LANES), lambda b, h, qi: (qi, 0)),
                     pl.BlockSpec((tq, LANES), lambda b, h, qi: (qi, 0)),
                     pl.BlockSpec((L, LANES), lambda b, h, qi: (0, 0)),
                     pl.BlockSpec((L, LANES), lambda b, h, qi: (0, 0))]
        args += [ck, cv, cos_t, sin_t, cos_t, sin_t]
    in_specs.append(pl.BlockSpec((1, LANES), lambda b, h, qi: (0, 0)))
    args.append(da_norm_g[li:li + 1])
    lam_init = 0.8 - 0.6 * math.exp(-0.3 * layer)
    return pl.pallas_call(
        functools.partial(_attn_body, li=lam_init, cached=cached, tq=tq),
        grid=(nseq, heads, nq),
        in_specs=in_specs,
        out_specs=pl.BlockSpec((tq, LANES),
                               lambda b, h, qi: (row0 // tq + b * nq + qi, hy_ch // LANES + h)),
        out_shape=_sds((t, d), BF16),
        input_output_aliases={0: 0},
        compiler_params=_cp(("arbitrary", "arbitrary", "arbitrary"), big=True),
        name="diff_attention",
    )(*args)


def _dw_tables_body(f_ref, g_ref, b_ref, *, rt, kp, taps):
    n = 2 * rt
    r = lax.broadcasted_iota(jnp.int32, (n, kp), 0)
    j = lax.broadcasted_iota(jnp.int32, (n, kp), 1)
    f_ref[...] = _dft_basis(r, j, rt)
    t = lax.broadcasted_iota(jnp.int32, (rt, n), 0)
    r2 = lax.broadcasted_iota(jnp.int32, (rt, n), 1)
    scale = jnp.where((r2 == 0) | (r2 == rt), 1.0 / n, 2.0 / n)
    shift = taps - 1 + HALO - taps // 2
    g_ref[...] = _dft_basis(r2, t + shift, rt) * scale
    r3 = lax.broadcasted_iota(jnp.int32, (n, LANES), 0)
    k3 = lax.broadcasted_iota(jnp.int32, (n, LANES), 1)
    b_ref[...] = _dft_basis(r3, (taps - 1 - k3) & (n - 1), rt)


def _dw_tables(rt, taps):
    kp = -(-(rt + 2 * HALO) // LANES) * LANES
    return pl.pallas_call(
        functools.partial(_dw_tables_body, rt=rt, kp=kp, taps=taps),
        out_shape=[_sds((2 * rt, kp), F32), _sds((rt, 2 * rt), F32), _sds((2 * rt, LANES), F32)],
        name="dwconv_tables",
    )()


def _dwconv_body(prev_ref, cur_ref, next_ref, w_ref, b_ref, lg_ref, lb_ref, f_ref, g_ref, bm_ref,
                 o_ref, buf, hd, *, rt, dims):
    t_ctx, l_ctx, l_lat = dims
    i = pl.program_id(0)

    @pl.when(i == 0)
    def _():
        hd[...] = jnp.dot(bm_ref[...], w_ref[0], precision=HIGHEST, preferred_element_type=F32)
        buf[rt + 2 * HALO:, :] = jnp.zeros((buf.shape[0] - rt - 2 * HALO, buf.shape[1]), F32)

    row0 = i * rt
    in_ctx = row0 < t_ctx
    seq_pos = jnp.where(in_ctx, row0 % l_ctx, (row0 - t_ctx) % l_lat)
    seq_len = jnp.where(in_ctx, l_ctx, l_lat)
    is_first = seq_pos == 0
    is_last = seq_pos + rt == seq_len
    buf[0:HALO, :] = jnp.where(is_first, 0.0, prev_ref[...])
    buf[HALO:HALO + rt, :] = cur_ref[...]
    buf[HALO + rt:2 * HALO + rt, :] = jnp.where(is_last, 0.0, next_ref[...])

    z = jnp.dot(f_ref[...], buf[...].astype(BF16), preferred_element_type=F32)
    p, q = z[0:rt], z[rt:2 * rt]
    hp, hq = hd[0:rt, :], hd[rt:2 * rt, :]
    row = lax.broadcasted_iota(jnp.int32, (rt, 1), 0)
    yr = jnp.where(row == 0, p * hp, p * hp - q * hq)
    yq = jnp.where(row == 0, q * hq, p * hq + q * hp)
    u = (jnp.dot(g_ref[:, 0:rt], yr.astype(BF16), preferred_element_type=F32)
         + jnp.dot(g_ref[:, rt:2 * rt], yq.astype(BF16), preferred_element_type=F32) + b_ref[0])
    mu = jnp.mean(u, axis=-1, keepdims=True)
    uc = u - mu
    var = jnp.mean(uc * uc, axis=-1, keepdims=True)
    y = uc * lax.rsqrt(var + EPS) * lg_ref[0] + lb_ref[0]
    o_ref[...] = (y * jax.nn.sigmoid(y)).astype(o_ref.dtype)


def _conformer_dwconv(u, li, cv_dw, cv_dwb, cv_ln_g, cv_ln_b, tabs, dims, rt):
    t, d = u.shape
    f_bf, g_bf, bm = tabs
    kp = f_bf.shape[1]
    hb = rt // HALO
    nhb = t // HALO
    w_pad = jnp.pad(cv_dw, ((0, 0), (0, LANES - cv_dw.shape[1]), (0, 0)))
    vec = lambda a: a.reshape(a.shape[0], 1, d)
    vspec = pl.BlockSpec((1, 1, d), lambda i: (li, 0, 0))
    full = lambda a: pl.BlockSpec(a.shape, lambda i: (0, 0))
    return pl.pallas_call(
        functools.partial(_dwconv_body, rt=rt, dims=dims),
        grid=(t // rt,),
        in_specs=[pl.BlockSpec((HALO, d), lambda i: (jnp.maximum(i * hb - 1, 0), 0)),
                  pl.BlockSpec((rt, d), lambda i: (i, 0)),
                  pl.BlockSpec((HALO, d), lambda i: (jnp.minimum((i + 1) * hb, nhb - 1), 0)),
                  pl.BlockSpec((1, LANES, d), lambda i: (li, 0, 0)),
                  vspec, vspec, vspec, full(f_bf), full(g_bf), full(bm)],
        out_specs=pl.BlockSpec((rt, d), lambda i: (i, 0)),
        out_shape=_sds((t, d), BF16),
        scratch_shapes=[pltpu.VMEM((kp, d), F32), pltpu.VMEM((2 * rt, d), F32)],
        compiler_params=_cp(("arbitrary",), big=True),
        name="conformer_dwconv",
    )(u, u, u, w_pad, vec(cv_dwb), vec(cv_ln_g), vec(cv_ln_b), f_bf, g_bf, bm)


def _scatter_body(pos_ref, pad_ref, h_ref, xs_ref, zrow, sem, zsem, *, rt, n_exp):
    i = pl.program_id(0)

    def issue(r, carry):
        for kk in range(TOP_K):
            p = pos_ref[(i * rt + r) * TOP_K + kk]
            pltpu.make_async_copy(h_ref.at[pl.ds(r, 1)], xs_ref.at[pl.ds(p, 1)], sem).start()
        return carry

    lax.fori_loop(0, rt, issue, 0)

    @pl.when(i == 0)
    def _():
        zrow[...] = jnp.zeros(zrow.shape, zrow.dtype)
        for e in range(n_exp):
            lo, hi = pad_ref[e], pad_ref[n_exp + e]

            def zissue(r, carry):
                pltpu.make_async_copy(zrow, xs_ref.at[pl.ds(r, 1)], zsem).start()
                return carry

            def zwait(r, carry):
                pltpu.make_async_copy(zrow, xs_ref.at[pl.ds(r, 1)], zsem).wait()
                return carry

            lax.fori_loop(lo, hi, zissue, 0)
            lax.fori_loop(lo, hi, zwait, 0)

    def drain(r, carry):
        for kk in range(TOP_K):
            pltpu.make_async_copy(h_ref.at[pl.ds(0, 1)], xs_ref.at[pl.ds(0, 1)], sem).wait()
        return carry

    lax.fori_loop(0, rt, drain, 0)


def _moe_scatter(h, pos, pad_ranges, n_rows, n_exp, rt):
    t, d = h.shape
    grid_spec = pltpu.PrefetchScalarGridSpec(
        num_scalar_prefetch=2, grid=(t // rt,),
        in_specs=[pl.BlockSpec((rt, d), lambda i, p, z: (i, 0))],
        out_specs=pl.BlockSpec(memory_space=pl.ANY),
        scratch_shapes=[pltpu.VMEM((1, d), h.dtype), pltpu.SemaphoreType.DMA,
                        pltpu.SemaphoreType.DMA])
    return pl.pallas_call(
        functools.partial(_scatter_body, rt=rt, n_exp=n_exp),
        grid_spec=grid_spec, out_shape=_sds((n_rows, d), h.dtype),
        compiler_params=_cp(("arbitrary",)),
        name="moe_scatter",
    )(pos, pad_ranges, h)


def _combine_body(pos_ref, x_ref, meta_ref, gate_ref, ys_ref, o_ref, buf, sem, *, rt):
    i = pl.program_id(0)

    def issue(r, carry):
        for kk in range(TOP_K):
            p = pos_ref[(i * rt + r) * TOP_K + kk]
            pltpu.make_async_copy(ys_ref.at[pl.ds(p, 1)], buf.at[kk, pl.ds(r, 1)], sem).start()
        return carry

    lax.fori_loop(0, rt, issue, 0)

    def drain(r, carry):
        for kk in range(TOP_K):
            pltpu.make_async_copy(ys_ref.at[pl.ds(0, 1)], buf.at[kk, pl.ds(0, 1)], sem).wait()
        return carry

    lax.fori_loop(0, rt, drain, 0)
    meta = meta_ref[...]
    f = meta[:, 2:3] * buf[0] + meta[:, 3:4] * buf[1]
    o_ref[...] = x_ref[...] + gate_ref[0] * f


def _moe_combine(x, ys, pos, meta, modr, layer, chunk, dims, rt):
    t, d = x.shape
    t_ctx, l_lat = dims
    grid_spec = pltpu.PrefetchScalarGridSpec(
        num_scalar_prefetch=1, grid=(t // rt,),
        in_specs=[pl.BlockSpec((rt, d), lambda i, p: (i, 0)),
                  pl.BlockSpec((rt, LANES), lambda i, p: (i, 0)),
                  pl.BlockSpec((1, 1, d), lambda i, p: (
                      layer * MOD_ROWS + _mod_row(i * rt, t_ctx, l_lat), 0, chunk)),
                  pl.BlockSpec(memory_space=pl.ANY)],
        out_specs=pl.BlockSpec((rt, d), lambda i, p: (i, 0)),
        scratch_shapes=[pltpu.VMEM((TOP_K, rt, d), F32), pltpu.SemaphoreType.DMA])
    return pl.pallas_call(
        functools.partial(_combine_body, rt=rt),
        grid_spec=grid_spec, out_shape=_sds((t, d), F32),
        compiler_params=_cp(("arbitrary",)),
        name="moe_combine",
    )(pos, x, meta, modr, ys)


def _routing_tables(meta, n_exp, tm, n_tiles):
    ids = meta[:, 0:TOP_K].astype(jnp.int32).reshape(-1)
    onehot = (ids[:, None] == jnp.arange(n_exp)[None, :]).astype(jnp.int32)
    csum = jnp.cumsum(onehot, axis=0)
    counts = csum[-1]
    rank = jnp.sum(onehot * csum, axis=1) - 1
    padded = (counts + tm - 1) // tm * tm
    ends = jnp.cumsum(padded)
    starts = ends - padded
    pos = (starts[ids] + rank).astype(jnp.int32)
    n_active = (ends[-1] // tm).astype(jnp.int32)
    tile_start = jnp.arange(n_tiles, dtype=jnp.int32) * tm
    te = jnp.sum((tile_start[:, None] >= ends[None, :]).astype(jnp.int32), axis=1)
    te_last = jnp.sum((jnp.maximum(ends[-1] - tm, 0) >= ends).astype(jnp.int32))
    te = jnp.where(tile_start < ends[-1], te, te_last).astype(jnp.int32)
    hi = ends.at[-1].set(n_tiles * tm)
    pad_ranges = jnp.concatenate([starts + counts, hi]).astype(jnp.int32)
    return pos, te, n_active.reshape(1), pad_ranges


def kernel(x_prompt, x_sample, cache_k, cache_v, c, c_ctx, norm_mix_g, norm_ffn_g, w_mod, b_mod,
           w_in, w_out, hy_short_w, hy_short_b, hy_w1, hy_b1, hy_fr1, hy_w2, hy_b2, hy_fr2, hy_w3,
           hy_fbias, da_lq1, da_lk1, da_lq2, da_lk2, da_norm_g, cv_w1, cv_b1, cv_dw, cv_dwb,
           cv_ln_g, cv_ln_b, cv_w2, cv_b2, ff_w1, ff_w3, ff_w2, moe_router, moe_w1, moe_w3, moe_w2,
           final_g):
    b_ctx, l_ctx, d = x_prompt.shape
    b_lat, l_lat, _ = x_sample.shape
    t_ctx, t_lat = b_ctx * l_ctx, b_lat * l_lat
    t = t_ctx + t_lat
    depth = w_mod.shape[0]
    hy_ch = hy_fbias.shape[-1]
    da_width = d - hy_ch
    heads = da_width // DA_VD
    n_exp = moe_router.shape[-1]
    d_ff = ff_w1.shape[-1]
    e_ff = moe_w1.shape[-1]
    past = cache_k.shape[2]
    n_even = w_in.shape[0]
    dims = (t_ctx, l_lat)
    rt = min(256, l_ctx)
    tm_big = min(1024, l_lat, t_ctx)
    tm_res = min(512, l_lat, t_ctx)
    tm_moe = min(512, t)
    assert b_lat + 1 <= MOD_ROWS and t_ctx % tm_big == 0 and l_lat % tm_big == 0

    x = jnp.concatenate([x_prompt.reshape(t_ctx, d), x_sample.reshape(t_lat, d)], axis=0)
    cond = jnp.zeros((MOD_ROWS, d), F32).at[0].set(c_ctx).at[1:1 + b_lat].set(c)
    modr = _mods(cond, w_mod, b_mod).reshape(depth * MOD_ROWS, 1, 6 * d)

    tables = {}
    for L in sorted({l_ctx, l_lat}):
        f_tab, g_tab = _dft_tables(L)
        tables[L] = (f_tab, f_tab.astype(BF16), g_tab.astype(BF16))
    dw_f, dw_g, dw_b = _dw_tables(rt, cv_dw.shape[1])
    dw_tabs = (dw_f.astype(BF16), dw_g.astype(BF16), dw_b)
    cos_t, sin_t = _rope_tables(l_lat)
    ck = cache_k.reshape(b_lat, n_even, past, heads * DA_VD)
    cv = cache_v.reshape(b_lat, n_even, past, heads * DA_VD)
    lams = (da_lq1, da_lk1, da_lq2, da_lk2)
    router_pad = jnp.pad(moe_router, ((0, 0), (0, 0), (0, LANES - n_exp)))
    moe_w1r = moe_w1.reshape(-1, d, e_ff)
    moe_w3r = moe_w3.reshape(-1, d, e_ff)
    moe_w2r = moe_w2.reshape(-1, e_ff, d)
    n_tiles = (t * TOP_K) // tm_moe + n_exp
    n_rows = n_tiles * tm_moe
    vec3 = lambda a: a.reshape(a.shape[0], 1, a.shape[-1])

    ks, vs = [], []
    for layer in range(depth):
        li = layer // 2
        if layer % 2 == 0:
            h = _norm_mod(x, norm_mix_g, layer, modr, 0, 1, dims, tm_res)
            proj = _matmul(h, [w_in], li, [0], w_in.shape[-1], tm=tm_big,
                           tn=min(1024, w_in.shape[-1]), name="in_proj")
            ymix = jnp.zeros((t, d), BF16)
            for (nseq, L, row0) in ((b_ctx, l_ctx, 0), (b_lat, l_lat, t_ctx)):
                f_tab, f_bf, g_bf = tables[L]
                hs = _hyena_filters(L, li, hy_w1, hy_b1, hy_fr1, hy_w2, hy_b2, hy_fr2, hy_w3,
                                    f_tab, hy_ch)
                ymix = _hyena(proj, ymix, li, nseq, L, row0 // L, hy_short_w, hy_short_b,
                              hy_fbias, hs, f_bf, g_bf, hy_ch)
            ymix = _attention(proj, ymix, li, layer, b_ctx, l_ctx, 0, lams, da_norm_g, hy_ch,
                              da_width)
            ymix = _attention(proj, ymix, li, layer, b_lat, l_lat, t_ctx, lams, da_norm_g, hy_ch,
                              da_width, cache=(ck, cv, cos_t, sin_t))
            kc0 = 3 * hy_ch + da_width
            ks.append(proj[:t_ctx, kc0:kc0 + da_width].reshape(b_ctx, l_ctx, heads, DA_VD))
            vs.append(proj[:t_ctx, kc0 + da_width:kc0 + 2 * da_width]
                      .reshape(b_ctx, l_ctx, heads, DA_VD))
            x = _matmul(ymix, [w_out], li, [0], d, tm=tm_big, tn=min(1024, d), epi="plain",
                        res=x, gate=(modr, layer, 2, dims), name="out_proj")
            h = _norm_mod(x, norm_ffn_g, layer, modr, 3, 4, dims, tm_res)
            hid = _matmul(h, [ff_w1, ff_w3], li, [0, 0], d_ff, tm=tm_big, tn=min(512, d_ff),
                          epi="swiglu", out_dtype=BF16, name="ffn_up")
            x = _matmul(hid, [ff_w2], li, [0], d, tm=tm_res, tn=min(512, d), res=x,
                        gate=(modr, layer, 5, dims), name="ffn_down")
        else:
            h = _norm_mod(x, norm_mix_g, layer, modr, 0, 1, dims, tm_res)
            u = _matmul(h, [cv_w1, cv_w1], li, [0, d], d, tm=tm_big, tn=min(512, d), epi="glu",
                        biases=[vec3(cv_b1), vec3(cv_b1)], bidx=li, name="conformer_up")
            u = _conformer_dwconv(u, li, cv_dw, cv_dwb, cv_ln_g, cv_ln_b, dw_tabs,
                                  (t_ctx, l_ctx, l_lat), rt)
            x = _matmul(u, [cv_w2], li, [0], d, tm=tm_big, tn=min(1024, d), biases=[vec3(cv_b2)],
                        bidx=li, res=x, gate=(modr, layer, 2, dims), name="conformer_down")
            hpk, meta = _norm_router(x, norm_ffn_g, layer, modr, 3, 4, router_pad[li], n_exp,
                                     dims, tm_res)
            pos, te, n_act, pad_ranges = _routing_tables(meta, n_exp, tm_moe, n_tiles)
            xs = _moe_scatter(hpk, pos, pad_ranges, n_rows, n_exp, rt)
            group = (te, n_act, li * n_exp)
            hid = _matmul(xs, [moe_w1r, moe_w3r], 0, [0, 0], e_ff, tm=tm_moe,
                          tn=min(1024, e_ff), epi="swiglu", out_dtype=BF16, group=group,
                          name="moe_up")
            ys = _matmul(hid, [moe_w2r], 0, [0], d, tm=tm_moe, tn=min(512, d), group=group,
                         name="moe_down")
            x = _moe_combine(x, ys, pos, meta, modr, layer, 5, dims, rt)

    y = _final_norm(x, final_g, tm_res)
    y_prompt = y[:t_ctx].reshape(b_ctx, l_ctx, d)
    y_sample = y[t_ctx:].reshape(b_lat, l_lat, d)
    return (y_prompt, y_sample, jnp.stack(ks, axis=1), jnp.stack(vs, axis=1))
```

```python
import functools
import math

import jax
import jax.numpy as jnp
from jax import lax
from jax.experimental import pallas as pl
from jax.experimental.pallas import tpu as pltpu

F32 = jnp.float32
BF16 = jnp.bfloat16
HIGHEST = lax.Precision.HIGHEST

EPS = 1e-6
GRID_W = 64
HY_EMB = 33
HY_FILT_HID = 64
HY_FAST_PCT = 0.3
HY_SLOW_PCT = 1.5
HY_TARGET = 1e-2
DA_DH = 64
DA_VD = 2 * DA_DH
ROPE_THETA = 10000.0
TOP_K = 2
MOD_ROWS = 16
LANES = 128
V7X_VMEM_LIMIT = 56 * 2**20
HALO = 16
ROW_CHUNKS = 4


def _cp(sem, big=False):
    return pltpu.CompilerParams(dimension_semantics=sem,
                                vmem_limit_bytes=V7X_VMEM_LIMIT if big else None)


def _sds(shape, dtype):
    return jax.ShapeDtypeStruct(shape, dtype)


def _fit(n, pref):
    best = LANES
    for w in range(LANES, min(n, pref) + 1, LANES):
        if n % w == 0:
            best = w
    return best


def _mod_row(row0, t_ctx, l_lat):
    return jnp.where(row0 < t_ctx, 0, 1 + (row0 - t_ctx) // l_lat)


def _mod_body(c_ref, w_ref, b_ref, o_ref):
    c = c_ref[...]
    s = (c * jax.nn.sigmoid(c)).astype(BF16)
    o_ref[0] = jnp.dot(s, w_ref[0].astype(BF16), preferred_element_type=F32) + b_ref[0]


def _mods(cond, w_mod, b_mod):
    depth, d, nm = w_mod.shape
    tn = _fit(nm, 1024)
    return pl.pallas_call(
        _mod_body,
        grid=(depth, nm // tn),
        in_specs=[pl.BlockSpec((MOD_ROWS, d), lambda l, n: (0, 0)),
                  pl.BlockSpec((1, d, tn), lambda l, n: (l, 0, n)),
                  pl.BlockSpec((1, 1, tn), lambda l, n: (l, 0, n))],
        out_specs=pl.BlockSpec((1, MOD_ROWS, tn), lambda l, n: (l, 0, n)),
        out_shape=_sds((depth, MOD_ROWS, nm), F32),
        compiler_params=_cp(("arbitrary", "arbitrary"), big=True),
        name="modulation",
    )(cond, w_mod, b_mod.reshape(depth, 1, nm))


def _rms(x, g):
    return x * lax.rsqrt(jnp.mean(x * x, axis=-1, keepdims=True) + EPS) * g


def _norm_mod_body(x_ref, g_ref, sh_ref, sc_ref, o_ref):
    y = _rms(x_ref[...], g_ref[0])
    o_ref[...] = (y * (1.0 + sc_ref[0]) + sh_ref[0]).astype(o_ref.dtype)


def _pack_halves(h):
    half = h.shape[1] // 2
    bits = lax.bitcast_convert_type(h.astype(BF16).astype(F32), jnp.uint32)
    return (bits[:, :half] >> 16) | (bits[:, half:] & jnp.uint32(0xFFFF0000))


def _unpack_halves(words):
    lo = lax.bitcast_convert_type(words << 16, F32).astype(BF16)
    hi = lax.bitcast_convert_type(words & jnp.uint32(0xFFFF0000), F32).astype(BF16)
    return lo, hi


def _norm_router_body(x_ref, g_ref, sh_ref, sc_ref, r_ref, h_ref, meta_ref, *, n_exp):
    y = _rms(x_ref[...], g_ref[0])
    h = y * (1.0 + sc_ref[0]) + sh_ref[0]
    h_ref[...] = _pack_halves(h)
    logits = jnp.dot(h, r_ref[...], precision=HIGHEST, preferred_element_type=F32)
    lane = lax.broadcasted_iota(jnp.int32, logits.shape, 1)
    neg = jnp.float32(-jnp.inf)
    logits = jnp.where(lane < n_exp, logits, neg)
    m1 = jnp.max(logits, axis=-1, keepdims=True)
    i1 = jnp.min(jnp.where(logits == m1, lane, LANES), axis=-1, keepdims=True)
    rest = jnp.where(lane == i1, neg, logits)
    m2 = jnp.max(rest, axis=-1, keepdims=True)
    i2 = jnp.min(jnp.where(rest == m2, lane, LANES), axis=-1, keepdims=True)
    e2 = jnp.exp(m2 - m1)
    g1 = 1.0 / (1.0 + e2)
    g2 = e2 * g1
    meta = jnp.where(lane == 0, i1.astype(F32),
                     jnp.where(lane == 1, i2.astype(F32),
                               jnp.where(lane == 2, g1, jnp.where(lane == 3, g2, 0.0))))
    meta_ref[...] = meta


def _final_norm_body(x_ref, g_ref, o_ref):
    o_ref[...] = _rms(x_ref[...], g_ref[...])


def _mod_spec(rt, d, layer, chunk, dims):
    t_ctx, l_lat = dims
    return pl.BlockSpec(
        (1, 1, d), lambda i: (layer * MOD_ROWS + _mod_row(i * rt, t_ctx, l_lat), 0, chunk))


def _norm_mod(x, g_all, layer, modr, sh_chunk, sc_chunk, dims, rt):
    t, d = x.shape
    return pl.pallas_call(
        _norm_mod_body,
        grid=(t // rt,),
        in_specs=[pl.BlockSpec((rt, d), lambda i: (i, 0)),
                  pl.BlockSpec((1, 1, d), lambda i: (layer, 0, 0)),
                  _mod_spec(rt, d, layer, sh_chunk, dims),
                  _mod_spec(rt, d, layer, sc_chunk, dims)],
        out_specs=pl.BlockSpec((rt, d), lambda i: (i, 0)),
        out_shape=_sds((t, d), BF16),
        compiler_params=_cp(("arbitrary",)),
        name="norm_modulate",
    )(x, g_all.reshape(-1, 1, d), modr, modr)


def _norm_router(x, g_all, layer, modr, sh_chunk, sc_chunk, router_pad, n_exp, dims, rt):
    t, d = x.shape
    return pl.pallas_call(
        functools.partial(_norm_router_body, n_exp=n_exp),
        grid=(t // rt,),
        in_specs=[pl.BlockSpec((rt, d), lambda i: (i, 0)),
                  pl.BlockSpec((1, 1, d), lambda i: (layer, 0, 0)),
                  _mod_spec(rt, d, layer, sh_chunk, dims),
                  _mod_spec(rt, d, layer, sc_chunk, dims),
                  pl.BlockSpec((d, LANES), lambda i: (0, 0))],
        out_specs=[pl.BlockSpec((rt, d // 2), lambda i: (i, 0)),
                   pl.BlockSpec((rt, LANES), lambda i: (i, 0))],
        out_shape=[_sds((t, d // 2), jnp.uint32), _sds((t, LANES), F32)],
        compiler_params=_cp(("arbitrary",)),
        name="norm_router",
    )(x, g_all.reshape(-1, 1, d), modr, modr, router_pad)


def _final_norm(x, g, rt):
    t, d = x.shape
    return pl.pallas_call(
        _final_norm_body,
        grid=(t // rt,),
        in_specs=[pl.BlockSpec((rt, d), lambda i: (i, 0)),
                  pl.BlockSpec((1, d), lambda i: (0, 0))],
        out_specs=pl.BlockSpec((rt, d), lambda i: (i, 0)),
        out_shape=_sds((t, d), F32),
        compiler_params=_cp(("arbitrary",)),
        name="final_norm",
    )(x, g.reshape(1, d))


def _mm_body(*refs, nw, has_bias, epi, grouped, has_res):
    it = iter(refs)
    if grouped:
        te_ref = next(it)
        nv_ref = next(it)
        next(it)
    x_ref = next(it)
    w_refs = [next(it) for _ in range(nw)]
    b_refs = [next(it) for _ in range(nw)] if has_bias else []
    if has_res:
        res_ref = next(it)
        gate_ref = next(it)
    o_ref = next(it)
    wc_refs = [next(it) for _ in range(nw)]

    i = pl.program_id(1)
    if grouped:
        recast = (i == 0) | (te_ref[i] != te_ref[jnp.maximum(i - 1, 0)])
    else:
        recast = i == 0

    @pl.when(recast)
    def _():
        for w_ref, wc in zip(w_refs, wc_refs):
            wc[...] = w_ref[0].astype(BF16)

    tm = x_ref.shape[0]

    def compute(rows=tm):
        if x_ref.dtype == jnp.uint32:
            lo, hi = _unpack_halves(x_ref[0:rows, :])
            half = lo.shape[1]
            accs = [jnp.dot(lo, wc[0:half, :], preferred_element_type=F32)
                    + jnp.dot(hi, wc[half:2 * half, :], preferred_element_type=F32)
                    for wc in wc_refs]
        else:
            x = x_ref[0:rows, :].astype(BF16)
            accs = [jnp.dot(x, wc[...], preferred_element_type=F32) for wc in wc_refs]
        if has_bias:
            accs = [a + b[0] for a, b in zip(accs, b_refs)]
        if epi == "plain":
            y = accs[0]
        elif epi == "swiglu":
            y = accs[0] * jax.nn.sigmoid(accs[0]) * accs[1]
        elif epi == "glu":
            y = accs[0] * jax.nn.sigmoid(accs[1])
        if has_res:
            y = res_ref[...] + gate_ref[0] * y
        o_ref[0:rows, :] = y.astype(o_ref.dtype)
        if rows < tm:
            o_ref[rows:tm, :] = jnp.zeros((tm - rows, o_ref.shape[1]), o_ref.dtype)

    if grouped:
        nv = nv_ref[i]
        for c in range(1, ROW_CHUNKS + 1):
            pl.when(nv == c)(functools.partial(compute, c * (tm // ROW_CHUNKS)))

        @pl.when(nv == 0)
        def _():
            o_ref[...] = jnp.zeros(o_ref.shape, o_ref.dtype)
    else:
        compute()


def _matmul(x, ws, widx, col0s, n_out, *, tm, tn, epi="plain", biases=None, bidx=0,
            res=None, gate=None, out_dtype=F32, group=None, name="matmul"):
    m, kx = x.shape
    k = ws[0].shape[1]
    nw = len(ws)
    grouped = group is not None
    has_bias = biases is not None
    has_res = res is not None
    tn = _fit(n_out, tn)
    grid = (n_out // tn, m // tm)

    if grouped:
        te, nv, na, base = group

        def wmap(c0):
            return lambda n, i, te_r, nv_r, na_r: (base + te_r[i], 0, c0 // tn + n)

        def rmap(f):
            return lambda n, i, te_r, nv_r, na_r: f(n, i)

        xmap = lambda n, i, te_r, nv_r, na_r: (jnp.minimum(i, na_r[0] - 1), 0)
    else:
        def wmap(c0):
            return lambda n, i: (widx, 0, c0 // tn + n)

        def rmap(f):
            return f

        xmap = lambda n, i: (i, 0)

    in_specs = [pl.BlockSpec((tm, kx), xmap)]
    args = [x]
    for w, c0 in zip(ws, col0s):
        in_specs.append(pl.BlockSpec((1, k, tn), wmap(c0)))
        args.append(w)
    if has_bias:
        for b, c0 in zip(biases, col0s):
            in_specs.append(pl.BlockSpec((1, 1, tn), rmap(
                lambda n, i, c0=c0: (bidx, 0, c0 // tn + n))))
            args.append(b)
    if has_res:
        modr, layer, chunk, (t_ctx, l_lat) = gate
        in_specs.append(pl.BlockSpec((tm, tn), rmap(lambda n, i: (i, n))))
        args.append(res)
        in_specs.append(pl.BlockSpec((1, 1, tn), rmap(
            lambda n, i: (layer * MOD_ROWS + _mod_row(i * tm, t_ctx, l_lat), 0,
                          chunk * (n_out // tn) + n))))
        args.append(modr)
    out_spec = pl.BlockSpec((tm, tn), rmap(lambda n, i: (i, n)))
    scratch = [pltpu.VMEM((k, tn), BF16) for _ in range(nw)]
    body = functools.partial(_mm_body, nw=nw, has_bias=has_bias, epi=epi, grouped=grouped,
                             has_res=has_res)
    if grouped:
        grid_spec = pltpu.PrefetchScalarGridSpec(
            num_scalar_prefetch=3, grid=grid, in_specs=in_specs, out_specs=out_spec,
            scratch_shapes=scratch)
        return pl.pallas_call(body, grid_spec=grid_spec, out_shape=_sds((m, n_out), out_dtype),
                              compiler_params=_cp(("arbitrary", "arbitrary"), big=True),
                              name=name)(te, nv, na, *args)
    return pl.pallas_call(body, grid=grid, in_specs=in_specs, out_specs=out_spec,
                          out_shape=_sds((m, n_out), out_dtype), scratch_shapes=scratch,
                          compiler_params=_cp(("arbitrary", "arbitrary"), big=True),
                          name=name)(*args)


def _dft_basis(r, s, L):
    n = 2 * L
    f = jnp.where(r < L, r, r - L)
    k = (f * s + jnp.where(r < L, 0, n // 4)) & (n - 1)
    val = jnp.cos(k.astype(F32) * (2.0 * math.pi / n))
    nyq = (1 - 2 * (s & 1)).astype(F32)
    return jnp.where(r == L, nyq, val)


def _dft_body(f_ref, g_ref, *, L, rb):
    i = pl.program_id(0)
    r = i * rb + lax.broadcasted_iota(jnp.int32, (rb, L), 0)
    s = lax.broadcasted_iota(jnp.int32, (rb, L), 1)
    f_ref[...] = _dft_basis(r, s, L)
    t = lax.broadcasted_iota(jnp.int32, (L, rb), 0)
    r2 = i * rb + lax.broadcasted_iota(jnp.int32, (L, rb), 1)
    scale = jnp.where((r2 == 0) | (r2 == L), 1.0 / (2 * L), 1.0 / L)
    g_ref[...] = _dft_basis(r2, t, L) * scale


def _dft_tables(L):
    rb = min(256, L)
    return pl.pallas_call(
        functools.partial(_dft_body, L=L, rb=rb),
        grid=(2 * L // rb,),
        out_specs=[pl.BlockSpec((rb, L), lambda i: (i, 0)),
                   pl.BlockSpec((L, rb), lambda i: (0, i))],
        out_shape=[_sds((2 * L, L), F32), _sds((L, 2 * L), F32)],
        compiler_params=_cp(("arbitrary",)),
        name="dft_tables",
    )()


def _filter_body(z_ref, w1_ref, b1_ref, fr1_ref, w2_ref, b2_ref, fr2_ref, w3a_ref, w3b_ref,
                 dl_ref, f_ref, o_ref, *, L):
    z = z_ref[...]
    dot = functools.partial(jnp.dot, precision=HIGHEST, preferred_element_type=F32)
    h = jnp.sin(fr1_ref[...] * (dot(z, w1_ref[0]) + b1_ref[...]))
    h = jnp.sin(fr2_ref[...] * (dot(h, w2_ref[0]) + b2_ref[...]))
    t = z[:, 0:1]
    decay = jnp.exp(-t * dl_ref[...])
    hf = dot(h, w3a_ref[0]) * decay
    hb = dot(h, w3b_ref[0]) * decay
    row = lax.broadcasted_iota(jnp.int32, (L, 1), 0)
    hb = jnp.where(row == 0, 0.0, hb)
    nrm = (jnp.sum(jnp.abs(hf), axis=0, keepdims=True)
           + jnp.sum(jnp.abs(hb), axis=0, keepdims=True))
    p = dot(f_ref[0:L, :], hf + hb)
    q = dot(f_ref[L:2 * L, :], hf - hb)
    sign = (1 - 2 * (row & 1)).astype(F32)
    nyq_fix = 2.0 * jnp.sum(sign * hb, axis=0, keepdims=True)
    q = q + jnp.where(row == 0, nyq_fix, 0.0)
    inv = 1.0 / nrm
    o_ref[0, 0:L, :] = p * inv
    o_ref[0, L:2 * L, :] = q * inv


def _pad2(a, rows, cols):
    return jnp.pad(a, ((0, rows - a.shape[0]), (0, cols - a.shape[1])))


def _hyena_filters(L, li, hy_w1, hy_b1, hy_fr1, hy_w2, hy_b2, hy_fr2, hy_w3, f_tab, hy_ch):
    bands = (HY_EMB - 1) // 2
    pos = jnp.arange(L, dtype=F32)
    t = jnp.linspace(0.0, 1.0, L, dtype=F32)
    f = jnp.linspace(1e-4, bands - 1, bands, dtype=F32)
    ang = 2.0 * math.pi * pos[:, None] * f[None, :] / L
    z = jnp.concatenate([t[:, None], jnp.cos(ang), -jnp.sin(ang)], axis=-1)
    z = _pad2(z, L, LANES)
    max_decay = math.log(HY_TARGET) / HY_FAST_PCT
    min_decay = math.log(HY_TARGET) / HY_SLOW_PCT
    deltas = jnp.abs(jnp.linspace(min_decay, max_decay, hy_ch, dtype=F32)).reshape(1, hy_ch)
    n_even = hy_w1.shape[0]
    w1 = jnp.pad(hy_w1, ((0, 0), (0, LANES - HY_EMB), (0, LANES - HY_FILT_HID)))
    w2 = jnp.pad(hy_w2, ((0, 0), (0, LANES - HY_FILT_HID), (0, LANES - HY_FILT_HID)))
    w3 = jnp.pad(hy_w3, ((0, 0), (0, LANES - HY_FILT_HID), (0, 0)))
    padv = lambda a: _pad2(a[li:li + 1], 1, LANES)
    cb = min(256, hy_ch)
    ncb = hy_ch // cb
    vec = pl.BlockSpec((1, LANES), lambda j: (0, 0))
    sq = pl.BlockSpec((1, LANES, LANES), lambda j: (li, 0, 0))
    return pl.pallas_call(
        functools.partial(_filter_body, L=L),
        grid=(2 * ncb,),
        in_specs=[pl.BlockSpec((L, LANES), lambda j: (0, 0)),
                  sq, vec, vec, sq, vec, vec,
                  pl.BlockSpec((1, LANES, cb), lambda j: (li, 0, j)),
                  pl.BlockSpec((1, LANES, cb), lambda j: (li, 0, 2 * ncb + j)),
                  pl.BlockSpec((1, cb), lambda j: (0, j % ncb)),
                  pl.BlockSpec((2 * L, L), lambda j: (0, 0))],
        out_specs=pl.BlockSpec((1, 2 * L, cb), lambda j: (j // ncb, 0, j % ncb)),
        out_shape=_sds((2, 2 * L, hy_ch), F32),
        compiler_params=_cp(("arbitrary",), big=True),
        name="hyena_filters",
    )(z, w1, padv(hy_b1), padv(hy_fr1), w2, padv(hy_b2), padv(hy_fr2), w3, w3, deltas, f_tab)


def _hyena_body(ymix_ref, v_ref, x1_ref, x2_ref, swv_ref, sw1_ref, sw2_ref, sbv_ref, sb1_ref,
                sb2_ref, fb_ref, hs_ref, f_ref, g_ref, o_ref, *, L, spb):
    del ymix_ref
    row = lax.broadcasted_iota(jnp.int32, (L, 1), 0)

    def short(x_ref, w_ref, b_ref, r0):
        x = x_ref[r0:r0 + L, :]
        w = w_ref[0]
        xm = jnp.where(row == 0, 0.0, pltpu.roll(x, 1, 0))
        xp = jnp.where(row == L - 1, 0.0, pltpu.roll(x, L - 1, 0))
        return xm * w[0:1] + x * w[1:2] + xp * w[2:3] + b_ref[0]

    def fconv(z, order):
        zf = jnp.dot(f_ref[...], z.astype(BF16), preferred_element_type=F32)
        p, q = zf[0:L], zf[L:2 * L]
        hp, hq = hs_ref[order, 0:L, :], hs_ref[order, L:2 * L, :]
        yr = jnp.where(row == 0, p * hp, p * hp - q * hq)
        yq = jnp.where(row == 0, q * hq, p * hq + q * hp)
        return (jnp.dot(g_ref[:, 0:L], yr.astype(BF16), preferred_element_type=F32)
                + jnp.dot(g_ref[:, L:2 * L], yq.astype(BF16), preferred_element_type=F32))

    fb = fb_ref[0]
    for s in range(spb):
        r0 = s * L
        v = short(v_ref, swv_ref, sbv_ref, r0)
        x1 = short(x1_ref, sw1_ref, sb1_ref, r0)
        x2 = short(x2_ref, sw2_ref, sb2_ref, r0)
        z1 = x1 * (fconv(v, 0) + fb[0:1] * v)
        o_ref[r0:r0 + L, :] = (x2 * (fconv(z1, 1) + fb[1:2] * z1)).astype(o_ref.dtype)


def _hyena(proj, ymix, li, nseq, L, row_blk0, hy_short_w, hy_short_b, hy_fbias, hs, f_bf, g_bf,
           hy_ch):
    t, d = ymix.shape
    cb = min(256, hy_ch)
    ncb = hy_ch // cb
    spb = 2 if nseq % 2 == 0 and row_blk0 % 2 == 0 else 1
    rb0 = row_blk0 // spb
    rowmap = lambda c: (lambda j, b: (rb0 + b, c * ncb + j))
    wmap = lambda c: (lambda j, b: (li, 0, c * ncb + j))
    sb = hy_short_b.reshape(hy_short_b.shape[0], 1, -1)
    in_specs = [pl.BlockSpec(memory_space=pl.ANY),
                pl.BlockSpec((spb * L, cb), rowmap(0)), pl.BlockSpec((spb * L, cb), rowmap(1)),
                pl.BlockSpec((spb * L, cb), rowmap(2)),
                pl.BlockSpec((1, 3, cb), wmap(0)), pl.BlockSpec((1, 3, cb), wmap(1)),
                pl.BlockSpec((1, 3, cb), wmap(2)),
                pl.BlockSpec((1, 1, cb), wmap(0)), pl.BlockSpec((1, 1, cb), wmap(1)),
                pl.BlockSpec((1, 1, cb), wmap(2)),
                pl.BlockSpec((1, 2, cb), lambda j, b: (li, 0, j)),
                pl.BlockSpec((2, 2 * L, cb), lambda j, b: (0, 0, j)),
                pl.BlockSpec((2 * L, L), lambda j, b: (0, 0)),
                pl.BlockSpec((L, 2 * L), lambda j, b: (0, 0))]
    args = [ymix, proj, proj, proj, hy_short_w, hy_short_w, hy_short_w, sb, sb, sb, hy_fbias, hs,
            f_bf, g_bf]
    return pl.pallas_call(
        functools.partial(_hyena_body, L=L, spb=spb),
        grid=(ncb, nseq // spb),
        in_specs=in_specs,
        out_specs=pl.BlockSpec((spb * L, cb), lambda j, b: (rb0 + b, j)),
        out_shape=_sds((t, d), BF16),
        input_output_aliases={0: 0},
        compiler_params=_cp(("arbitrary", "arbitrary"), big=True),
        name="hyena_mixer",
    )(*args)


def _rope_tables(L):
    half = DA_DH // 2
    nf = half // 2
    lane = jnp.arange(LANES)
    sub = lane % DA_DH
    idx = sub % half
    inv = ROPE_THETA ** (-(idx % nf).astype(F32) / nf)
    pos = jnp.arange(L)
    p = jnp.where((sub // half)[None, :] == 0, (pos // GRID_W)[:, None], (pos % GRID_W)[:, None])
    ang = p.astype(F32) * inv[None, :]
    sgn = jnp.where(idx < nf, -1.0, 1.0)[None, :]
    return jnp.cos(ang), jnp.sin(ang) * sgn


def _attn_body(*refs, li, cached, tq, L, unroll):
    it = iter(refs)
    next(it)
    lq1, lk1, lq2, lk2 = [next(it) for _ in range(4)]
    q_ref, k_ref, v_ref = next(it), next(it), next(it)
    if cached:
        ck_ref, cv_ref, cos_ref, sin_ref = [next(it) for _ in range(4)]
    g_ref = next(it)
    o_ref = next(it)

    lam = (jnp.exp(jnp.sum(lq1[...] * lk1[...], keepdims=True))
           - jnp.exp(jnp.sum(lq2[...] * lk2[...], keepdims=True)) + li)
    lane = lax.broadcasted_iota(jnp.int32, (1, LANES), 1)
    first = lane < DA_DH
    nf = DA_DH // 4

    def rope(x, c, s):
        swap = jnp.where((lane % (2 * nf)) < nf, pltpu.roll(x, LANES - nf, 1), pltpu.roll(x, nf, 1))
        return x * c + swap * s

    k = k_ref[...]
    if cached:
        k = rope(k, cos_ref[...], sin_ref[...])
        ckb = ck_ref[0, 0].astype(BF16)
        cvb = cv_ref[0, 0].astype(BF16)
    kb = k.astype(BF16)
    vb = v_ref[...].astype(BF16)
    nt = (((1,), (1,)), ((), ()))

    def softmax_pv(qm):
        s = lax.dot_general(qm, kb, nt, preferred_element_type=F32)
        m = jnp.max(s, axis=-1, keepdims=True)
        if cached:
            sc = lax.dot_general(qm, ckb, nt, preferred_element_type=F32)
            m = jnp.maximum(m, jnp.max(sc, axis=-1, keepdims=True))
        e = jnp.exp(s - m)
        den = jnp.sum(e, axis=-1, keepdims=True)
        o = jnp.dot(e.astype(BF16), vb, preferred_element_type=F32)
        if cached:
            ec = jnp.exp(sc - m)
            den = den + jnp.sum(ec, axis=-1, keepdims=True)
            o = o + jnp.dot(ec.astype(BF16), cvb, preferred_element_type=F32)
        return o * (1.0 / den)

    def q_block(r0):
        q = q_ref[pl.ds(r0, tq), :]
        if cached:
            q = rope(q, cos_ref[pl.ds(r0, tq), :], sin_ref[pl.ds(r0, tq), :])
        q = q * (DA_DH ** -0.5)
        o1 = softmax_pv(jnp.where(first, q, 0.0).astype(BF16))
        o2 = softmax_pv(jnp.where(first, 0.0, q).astype(BF16))
        o = _rms(o1 - lam * o2, g_ref[0]) * (1.0 - li)
        o_ref[pl.ds(r0, tq), :] = o.astype(o_ref.dtype)

    blocks = min(unroll, L // tq)

    def q_blocks(j, carry):
        for u in range(blocks):
            q_block(pl.multiple_of((j * blocks + u) * tq, tq))
        return carry

    lax.fori_loop(0, L // (tq * blocks), q_blocks, 0)


def _attention(proj, ymix, li, layer, nseq, L, row0, lams, da_norm_g, hy_ch, da_width,
               cache=None):
    t, d = ymix.shape
    heads = da_width // DA_VD
    cached = cache is not None
    tq = min(256, L)
    qc0 = 3 * hy_ch // LANES
    kc0 = qc0 + da_width // LANES
    vc0 = kc0 + da_width // LANES
    lam_spec = pl.BlockSpec((1, DA_DH), lambda b, h: (0, 0))
    seq = lambda c0: pl.BlockSpec((L, LANES), lambda b, h: (row0 // L + b, c0 + h))
    in_specs = [pl.BlockSpec(memory_space=pl.ANY)] + [lam_spec] * 4 + [seq(qc0), seq(kc0), seq(vc0)]
    args = [ymix] + [a[li:li + 1] for a in lams] + [proj, proj, proj]
    if cached:
        ck, cv, cos_t, sin_t = cache
        past = ck.shape[2]
        in_specs += [pl.BlockSpec((1, 1, past, LANES), lambda b, h: (b, li, 0, h)),
                     pl.BlockSpec((1, 1, past, LANES), lambda b, h: (b, li, 0, h)),
                     pl.BlockSpec((L, LANES), lambda b, h: (0, 0)),
                     pl.BlockSpec((L, LANES), lambda b, h: (0, 0))]
        args += [ck, cv, cos_t, sin_t]
    in_specs.append(pl.BlockSpec((1, LANES), lambda b, h: (0, 0)))
    args.append(da_norm_g[li:li + 1])
    lam_init = 0.8 - 0.6 * math.exp(-0.3 * layer)
    return pl.pallas_call(
        functools.partial(_attn_body, li=lam_init, cached=cached, tq=tq, L=L, unroll=2),
        grid=(nseq, heads),
        in_specs=in_specs,
        out_specs=seq(hy_ch // LANES),
        out_shape=_sds((t, d), BF16),
        input_output_aliases={0: 0},
        compiler_params=_cp(("arbitrary", "arbitrary"), big=True),
        name="diff_attention",
    )(*args)


def _dw_tables_body(f_ref, g_ref, b_ref, *, rt, kp, taps):
    n = 2 * rt
    r = lax.broadcasted_iota(jnp.int32, (n, kp), 0)
    j = lax.broadcasted_iota(jnp.int32, (n, kp), 1)
    f_ref[...] = _dft_basis(r, j, rt)
    t = lax.broadcasted_iota(jnp.int32, (rt, n), 0)
    r2 = lax.broadcasted_iota(jnp.int32, (rt, n), 1)
    scale = jnp.where((r2 == 0) | (r2 == rt), 1.0 / n, 2.0 / n)
    shift = taps - 1 + HALO - taps // 2
    g_ref[...] = _dft_basis(r2, t + shift, rt) * scale
    r3 = lax.broadcasted_iota(jnp.int32, (n, LANES), 0)
    k3 = lax.broadcasted_iota(jnp.int32, (n, LANES), 1)
    b_ref[...] = _dft_basis(r3, (taps - 1 - k3) & (n - 1), rt)


def _dw_tables(rt, taps):
    kp = -(-(rt + 2 * HALO) // LANES) * LANES
    return pl.pallas_call(
        functools.partial(_dw_tables_body, rt=rt, kp=kp, taps=taps),
        out_shape=[_sds((2 * rt, kp), F32), _sds((rt, 2 * rt), F32), _sds((2 * rt, LANES), F32)],
        name="dwconv_tables",
    )()


def _dwconv_body(prev_ref, cur_ref, next_ref, w_ref, b_ref, lg_ref, lb_ref, f_ref, g_ref, bm_ref,
                 o_ref, buf, hd, *, rt, dims):
    t_ctx, l_ctx, l_lat = dims
    i = pl.program_id(0)

    @pl.when(i == 0)
    def _():
        hd[...] = jnp.dot(bm_ref[...], w_ref[0], precision=HIGHEST, preferred_element_type=F32)
        buf[rt + 2 * HALO:, :] = jnp.zeros((buf.shape[0] - rt - 2 * HALO, buf.shape[1]), F32)

    row0 = i * rt
    in_ctx = row0 < t_ctx
    seq_pos = jnp.where(in_ctx, row0 % l_ctx, (row0 - t_ctx) % l_lat)
    seq_len = jnp.where(in_ctx, l_ctx, l_lat)
    is_first = seq_pos == 0
    is_last = seq_pos + rt == seq_len
    buf[0:HALO, :] = jnp.where(is_first, 0.0, prev_ref[...])
    buf[HALO:HALO + rt, :] = cur_ref[...]
    buf[HALO + rt:2 * HALO + rt, :] = jnp.where(is_last, 0.0, next_ref[...])

    z = jnp.dot(f_ref[...], buf[...].astype(BF16), preferred_element_type=F32)
    p, q = z[0:rt], z[rt:2 * rt]
    hp, hq = hd[0:rt, :], hd[rt:2 * rt, :]
    row = lax.broadcasted_iota(jnp.int32, (rt, 1), 0)
    yr = jnp.where(row == 0, p * hp, p * hp - q * hq)
    yq = jnp.where(row == 0, q * hq, p * hq + q * hp)
    u = (jnp.dot(g_ref[:, 0:rt], yr.astype(BF16), preferred_element_type=F32)
         + jnp.dot(g_ref[:, rt:2 * rt], yq.astype(BF16), preferred_element_type=F32) + b_ref[0])
    mu = jnp.mean(u, axis=-1, keepdims=True)
    uc = u - mu
    var = jnp.mean(uc * uc, axis=-1, keepdims=True)
    y = uc * lax.rsqrt(var + EPS) * lg_ref[0] + lb_ref[0]
    o_ref[...] = (y * jax.nn.sigmoid(y)).astype(o_ref.dtype)


def _conformer_dwconv(u, li, cv_dw, cv_dwb, cv_ln_g, cv_ln_b, tabs, dims, rt):
    t, d = u.shape
    f_bf, g_bf, bm = tabs
    kp = f_bf.shape[1]
    hb = rt // HALO
    nhb = t // HALO
    w_pad = jnp.pad(cv_dw, ((0, 0), (0, LANES - cv_dw.shape[1]), (0, 0)))
    vec = lambda a: a.reshape(a.shape[0], 1, d)
    vspec = pl.BlockSpec((1, 1, d), lambda i: (li, 0, 0))
    full = lambda a: pl.BlockSpec(a.shape, lambda i: (0, 0))
    return pl.pallas_call(
        functools.partial(_dwconv_body, rt=rt, dims=dims),
        grid=(t // rt,),
        in_specs=[pl.BlockSpec((HALO, d), lambda i: (jnp.maximum(i * hb - 1, 0), 0)),
                  pl.BlockSpec((rt, d), lambda i: (i, 0)),
                  pl.BlockSpec((HALO, d), lambda i: (jnp.minimum((i + 1) * hb, nhb - 1), 0)),
                  pl.BlockSpec((1, LANES, d), lambda i: (li, 0, 0)),
                  vspec, vspec, vspec, full(f_bf), full(g_bf), full(bm)],
        out_specs=pl.BlockSpec((rt, d), lambda i: (i, 0)),
        out_shape=_sds((t, d), BF16),
        scratch_shapes=[pltpu.VMEM((kp, d), F32), pltpu.VMEM((2 * rt, d), F32)],
        compiler_params=_cp(("arbitrary",), big=True),
        name="conformer_dwconv",
    )(u, u, u, w_pad, vec(cv_dwb), vec(cv_ln_g), vec(cv_ln_b), f_bf, g_bf, bm)


def _scatter_body(pos_ref, pad_ref, h_ref, xs_ref, zrow, sem, zsem, *, rt, n_exp):
    i = pl.program_id(0)

    def issue(r, carry):
        for kk in range(TOP_K):
            p = pos_ref[(i * rt + r) * TOP_K + kk]
            pltpu.make_async_copy(h_ref.at[pl.ds(r, 1)], xs_ref.at[pl.ds(p, 1)], sem).start()
        return carry

    lax.fori_loop(0, rt, issue, 0)

    @pl.when(i == 0)
    def _():
        zrow[...] = jnp.zeros(zrow.shape, zrow.dtype)
        for e in range(n_exp):
            lo, hi = pad_ref[e], pad_ref[n_exp + e]

            def zissue(r, carry):
                pltpu.make_async_copy(zrow, xs_ref.at[pl.ds(r, 1)], zsem).start()
                return carry

            def zwait(r, carry):
                pltpu.make_async_copy(zrow, xs_ref.at[pl.ds(r, 1)], zsem).wait()
                return carry

            lax.fori_loop(lo, hi, zissue, 0)
            lax.fori_loop(lo, hi, zwait, 0)

    def drain(r, carry):
        for kk in range(TOP_K):
            pltpu.make_async_copy(h_ref.at[pl.ds(0, 1)], xs_ref.at[pl.ds(0, 1)], sem).wait()
        return carry

    lax.fori_loop(0, rt, drain, 0)


def _moe_scatter(h, pos, pad_ranges, n_rows, n_exp, rt):
    t, d = h.shape
    grid_spec = pltpu.PrefetchScalarGridSpec(
        num_scalar_prefetch=2, grid=(t // rt,),
        in_specs=[pl.BlockSpec((rt, d), lambda i, p, z: (i, 0))],
        out_specs=pl.BlockSpec(memory_space=pl.ANY),
        scratch_shapes=[pltpu.VMEM((1, d), h.dtype), pltpu.SemaphoreType.DMA,
                        pltpu.SemaphoreType.DMA])
    return pl.pallas_call(
        functools.partial(_scatter_body, rt=rt, n_exp=n_exp),
        grid_spec=grid_spec, out_shape=_sds((n_rows, d), h.dtype),
        compiler_params=_cp(("arbitrary",)),
        name="moe_scatter",
    )(pos, pad_ranges, h)


def _combine_body(pos_ref, x_ref, meta_ref, gate_ref, ys_ref, o_ref, buf, sem, *, rt):
    i = pl.program_id(0)

    def issue(r, carry):
        for kk in range(TOP_K):
            p = pos_ref[(i * rt + r) * TOP_K + kk]
            pltpu.make_async_copy(ys_ref.at[pl.ds(p, 1)], buf.at[kk, pl.ds(r, 1)], sem).start()
        return carry

    lax.fori_loop(0, rt, issue, 0)

    def drain(r, carry):
        for kk in range(TOP_K):
            pltpu.make_async_copy(ys_ref.at[pl.ds(0, 1)], buf.at[kk, pl.ds(0, 1)], sem).wait()
        return carry

    lax.fori_loop(0, rt, drain, 0)
    meta = meta_ref[...]
    f = meta[:, 2:3] * buf[0] + meta[:, 3:4] * buf[1]
    o_ref[...] = x_ref[...] + gate_ref[0] * f


def _moe_combine(x, ys, pos, meta, modr, layer, chunk, dims, rt):
    t, d = x.shape
    t_ctx, l_lat = dims
    grid_spec = pltpu.PrefetchScalarGridSpec(
        num_scalar_prefetch=1, grid=(t // rt,),
        in_specs=[pl.BlockSpec((rt, d), lambda i, p: (i, 0)),
                  pl.BlockSpec((rt, LANES), lambda i, p: (i, 0)),
                  pl.BlockSpec((1, 1, d), lambda i, p: (
                      layer * MOD_ROWS + _mod_row(i * rt, t_ctx, l_lat), 0, chunk)),
                  pl.BlockSpec(memory_space=pl.ANY)],
        out_specs=pl.BlockSpec((rt, d), lambda i, p: (i, 0)),
        scratch_shapes=[pltpu.VMEM((TOP_K, rt, d), F32), pltpu.SemaphoreType.DMA])
    return pl.pallas_call(
        functools.partial(_combine_body, rt=rt),
        grid_spec=grid_spec, out_shape=_sds((t, d), F32),
        compiler_params=_cp(("arbitrary",)),
        name="moe_combine",
    )(pos, x, meta, modr, ys)


def _routing_tables(meta, n_exp, tm, n_tiles):
    ids = meta[:, 0:TOP_K].astype(jnp.int32).reshape(-1)
    onehot = (ids[:, None] == jnp.arange(n_exp)[None, :]).astype(jnp.int32)
    csum = jnp.cumsum(onehot, axis=0)
    counts = csum[-1]
    rank = jnp.sum(onehot * csum, axis=1) - 1
    padded = (counts + tm - 1) // tm * tm
    ends = jnp.cumsum(padded)
    starts = ends - padded
    pos = (starts[ids] + rank).astype(jnp.int32)
    n_active = (ends[-1] // tm).astype(jnp.int32)
    tile_start = jnp.arange(n_tiles, dtype=jnp.int32) * tm
    te = jnp.sum((tile_start[:, None] >= ends[None, :]).astype(jnp.int32), axis=1)
    te_last = jnp.sum((jnp.maximum(ends[-1] - tm, 0) >= ends).astype(jnp.int32))
    active = tile_start < ends[-1]
    te = jnp.where(active, te, te_last).astype(jnp.int32)
    chunk = tm // ROW_CHUNKS
    real = jnp.clip((starts + counts)[te] - tile_start, 0, tm)
    nv = jnp.where(active, (real + chunk - 1) // chunk, 0).astype(jnp.int32)
    hi = ends.at[-1].set(n_tiles * tm)
    pad_ranges = jnp.concatenate([starts + counts, hi]).astype(jnp.int32)
    return pos, te, nv, n_active.reshape(1), pad_ranges


def kernel(x_prompt, x_sample, cache_k, cache_v, c, c_ctx, norm_mix_g, norm_ffn_g, w_mod, b_mod,
           w_in, w_out, hy_short_w, hy_short_b, hy_w1, hy_b1, hy_fr1, hy_w2, hy_b2, hy_fr2, hy_w3,
           hy_fbias, da_lq1, da_lk1, da_lq2, da_lk2, da_norm_g, cv_w1, cv_b1, cv_dw, cv_dwb,
           cv_ln_g, cv_ln_b, cv_w2, cv_b2, ff_w1, ff_w3, ff_w2, moe_router, moe_w1, moe_w3, moe_w2,
           final_g):
    b_ctx, l_ctx, d = x_prompt.shape
    b_lat, l_lat, _ = x_sample.shape
    t_ctx, t_lat = b_ctx * l_ctx, b_lat * l_lat
    t = t_ctx + t_lat
    depth = w_mod.shape[0]
    hy_ch = hy_fbias.shape[-1]
    da_width = d - hy_ch
    heads = da_width // DA_VD
    n_exp = moe_router.shape[-1]
    d_ff = ff_w1.shape[-1]
    e_ff = moe_w1.shape[-1]
    past = cache_k.shape[2]
    n_even = w_in.shape[0]
    dims = (t_ctx, l_lat)
    rt = min(256, l_ctx)
    tm_big = min(1024, l_lat, t_ctx)
    tm_res = min(512, l_lat, t_ctx)
    tm_moe = min(512, t)
    assert b_lat + 1 <= MOD_ROWS and t_ctx % tm_big == 0 and l_lat % tm_big == 0

    x = jnp.concatenate([x_prompt.reshape(t_ctx, d), x_sample.reshape(t_lat, d)], axis=0)
    cond = jnp.zeros((MOD_ROWS, d), F32).at[0].set(c_ctx).at[1:1 + b_lat].set(c)
    modr = _mods(cond, w_mod, b_mod).reshape(depth * MOD_ROWS, 1, 6 * d)

    tables = {}
    for L in sorted({l_ctx, l_lat}):
        f_tab, g_tab = _dft_tables(L)
        tables[L] = (f_tab, f_tab.astype(BF16), g_tab.astype(BF16))
    dw_f, dw_g, dw_b = _dw_tables(rt, cv_dw.shape[1])
    dw_tabs = (dw_f.astype(BF16), dw_g.astype(BF16), dw_b)
    cos_t, sin_t = _rope_tables(l_lat)
    ck = cache_k.reshape(b_lat, n_even, past, heads * DA_VD)
    cv = cache_v.reshape(b_lat, n_even, past, heads * DA_VD)
    lams = (da_lq1, da_lk1, da_lq2, da_lk2)
    router_pad = jnp.pad(moe_router, ((0, 0), (0, 0), (0, LANES - n_exp)))
    moe_w1r = moe_w1.reshape(-1, d, e_ff)
    moe_w3r = moe_w3.reshape(-1, d, e_ff)
    moe_w2r = moe_w2.reshape(-1, e_ff, d)
    n_tiles = (t * TOP_K) // tm_moe + n_exp
    n_rows = n_tiles * tm_moe
    vec3 = lambda a: a.reshape(a.shape[0], 1, a.shape[-1])

    ks, vs = [], []
    for layer in range(depth):
        li = layer // 2
        if layer % 2 == 0:
            h = _norm_mod(x, norm_mix_g, layer, modr, 0, 1, dims, tm_res)
            proj = _matmul(h, [w_in], li, [0], w_in.shape[-1], tm=tm_big,
                           tn=min(1024, w_in.shape[-1]), name="in_proj")
            ymix = jnp.zeros((t, d), BF16)
            for (nseq, L, row0) in ((b_ctx, l_ctx, 0), (b_lat, l_lat, t_ctx)):
                f_tab, f_bf, g_bf = tables[L]
                hs = _hyena_filters(L, li, hy_w1, hy_b1, hy_fr1, hy_w2, hy_b2, hy_fr2, hy_w3,
                                    f_tab, hy_ch)
                ymix = _hyena(proj, ymix, li, nseq, L, row0 // L, hy_short_w, hy_short_b,
                              hy_fbias, hs, f_bf, g_bf, hy_ch)
            ymix = _attention(proj, ymix, li, layer, b_ctx, l_ctx, 0, lams, da_norm_g, hy_ch,
                              da_width)
            ymix = _attention(proj, ymix, li, layer, b_lat, l_lat, t_ctx, lams, da_norm_g, hy_ch,
                              da_width, cache=(ck, cv, cos_t, sin_t))
            kc0 = 3 * hy_ch + da_width
            ks.append(proj[:t_ctx, kc0:kc0 + da_width].reshape(b_ctx, l_ctx, heads, DA_VD))
            vs.append(proj[:t_ctx, kc0 + da_width:kc0 + 2 * da_width]
                      .reshape(b_ctx, l_ctx, heads, DA_VD))
            x = _matmul(ymix, [w_out], li, [0], d, tm=tm_big, tn=min(1024, d), epi="plain",
                        res=x, gate=(modr, layer, 2, dims), name="out_proj")
            h = _norm_mod(x, norm_ffn_g, layer, modr, 3, 4, dims, tm_res)
            hid = _matmul(h, [ff_w1, ff_w3], li, [0, 0], d_ff, tm=tm_big, tn=min(512, d_ff),
                          epi="swiglu", out_dtype=BF16, name="ffn_up")
            x = _matmul(hid, [ff_w2], li, [0], d, tm=tm_res, tn=min(512, d), res=x,
                        gate=(modr, layer, 5, dims), name="ffn_down")
        else:
            h = _norm_mod(x, norm_mix_g, layer, modr, 0, 1, dims, tm_res)
            u = _matmul(h, [cv_w1, cv_w1], li, [0, d], d, tm=tm_big, tn=min(512, d), epi="glu",
                        biases=[vec3(cv_b1), vec3(cv_b1)], bidx=li, name="conformer_up")
            u = _conformer_dwconv(u, li, cv_dw, cv_dwb, cv_ln_g, cv_ln_b, dw_tabs,
                                  (t_ctx, l_ctx, l_lat), rt)
            x = _matmul(u, [cv_w2], li, [0], d, tm=tm_big, tn=min(1024, d), biases=[vec3(cv_b2)],
                        bidx=li, res=x, gate=(modr, layer, 2, dims), name="conformer_down")
            hpk, meta = _norm_router(x, norm_ffn_g, layer, modr, 3, 4, router_pad[li], n_exp,
                                     dims, tm_res)
            pos, te, nv, n_act, pad_ranges = _routing_tables(meta, n_exp, tm_moe, n_tiles)
            xs = _moe_scatter(hpk, pos, pad_ranges, n_rows, n_exp, rt)
            group = (te, nv, n_act, li * n_exp)
            hid = _matmul(xs, [moe_w1r, moe_w3r], 0, [0, 0], e_ff, tm=tm_moe,
                          tn=min(1024, e_ff), epi="swiglu", out_dtype=BF16, group=group,
                          name="moe_up")
            ys = _matmul(hid, [moe_w2r], 0, [0], d, tm=tm_moe, tn=min(512, d), group=group,
                         name="moe_down")
            x = _moe_combine(x, ys, pos, meta, modr, layer, 5, dims, rt)

    y = _final_norm(x, final_g, tm_res)
    y_prompt = y[:t_ctx].reshape(b_ctx, l_ctx, d)
    y_sample = y[t_ctx:].reshape(b_lat, l_lat, d)
    return (y_prompt, y_sample, jnp.stack(ks, axis=1), jnp.stack(vs, axis=1))
```

```python
import functools
import math

import jax
import jax.numpy as jnp
from jax import lax
from jax.experimental import pallas as pl
from jax.experimental.pallas import tpu as pltpu

F32 = jnp.float32
BF16 = jnp.bfloat16
HIGHEST = lax.Precision.HIGHEST

EPS = 1e-6
GRID_W = 64
HY_EMB = 33
HY_FILT_HID = 64
HY_FAST_PCT = 0.3
HY_SLOW_PCT = 1.5
HY_TARGET = 1e-2
DA_DH = 64
DA_VD = 2 * DA_DH
ROPE_THETA = 10000.0
TOP_K = 2
MOD_ROWS = 16
LANES = 128
V7X_VMEM_LIMIT = 56 * 2**20
HALO = 16
DMA_UNROLL = 8
ROW_CHUNKS = 4


def _cp(sem, big=False):
    return pltpu.CompilerParams(dimension_semantics=sem,
                                vmem_limit_bytes=V7X_VMEM_LIMIT if big else None)


def _sds(shape, dtype):
    return jax.ShapeDtypeStruct(shape, dtype)


def _fit(n, pref):
    best = LANES
    for w in range(LANES, min(n, pref) + 1, LANES):
        if n % w == 0:
            best = w
    return best


def _mod_row(row0, t_ctx, l_lat):
    return jnp.where(row0 < t_ctx, 0, 1 + (row0 - t_ctx) // l_lat)


def _mod_body(c_ref, w_ref, b_ref, o_ref):
    c = c_ref[...]
    s = (c * jax.nn.sigmoid(c)).astype(BF16)
    o_ref[0] = jnp.dot(s, w_ref[0].astype(BF16), preferred_element_type=F32) + b_ref[0]


def _mods(cond, w_mod, b_mod):
    depth, d, nm = w_mod.shape
    tn = _fit(nm, 1024)
    return pl.pallas_call(
        _mod_body,
        grid=(depth, nm // tn),
        in_specs=[pl.BlockSpec((MOD_ROWS, d), lambda l, n: (0, 0)),
                  pl.BlockSpec((1, d, tn), lambda l, n: (l, 0, n)),
                  pl.BlockSpec((1, 1, tn), lambda l, n: (l, 0, n))],
        out_specs=pl.BlockSpec((1, MOD_ROWS, tn), lambda l, n: (l, 0, n)),
        out_shape=_sds((depth, MOD_ROWS, nm), F32),
        compiler_params=_cp(("arbitrary", "arbitrary"), big=True),
        name="modulation",
    )(cond, w_mod, b_mod.reshape(depth, 1, nm))


def _rms(x, g):
    return x * lax.rsqrt(jnp.mean(x * x, axis=-1, keepdims=True) + EPS) * g


def _norm_mod_body(x_ref, g_ref, sh_ref, sc_ref, o_ref):
    y = _rms(x_ref[...], g_ref[0])
    o_ref[...] = (y * (1.0 + sc_ref[0]) + sh_ref[0]).astype(o_ref.dtype)


def _pack_halves(h):
    half = h.shape[1] // 2
    bits = lax.bitcast_convert_type(h.astype(BF16).astype(F32), jnp.uint32)
    return (bits[:, :half] >> 16) | (bits[:, half:] & jnp.uint32(0xFFFF0000))


def _unpack_halves(words):
    lo = lax.bitcast_convert_type(words << 16, F32).astype(BF16)
    hi = lax.bitcast_convert_type(words & jnp.uint32(0xFFFF0000), F32).astype(BF16)
    return lo, hi


def _norm_router_body(x_ref, g_ref, sh_ref, sc_ref, r_ref, h_ref, meta_ref, *, n_exp):
    y = _rms(x_ref[...], g_ref[0])
    h = y * (1.0 + sc_ref[0]) + sh_ref[0]
    h_ref[...] = _pack_halves(h)
    logits = jnp.dot(h, r_ref[...], precision=HIGHEST, preferred_element_type=F32)
    lane = lax.broadcasted_iota(jnp.int32, logits.shape, 1)
    neg = jnp.float32(-jnp.inf)
    logits = jnp.where(lane < n_exp, logits, neg)
    m1 = jnp.max(logits, axis=-1, keepdims=True)
    i1 = jnp.min(jnp.where(logits == m1, lane, LANES), axis=-1, keepdims=True)
    rest = jnp.where(lane == i1, neg, logits)
    m2 = jnp.max(rest, axis=-1, keepdims=True)
    i2 = jnp.min(jnp.where(rest == m2, lane, LANES), axis=-1, keepdims=True)
    e2 = jnp.exp(m2 - m1)
    g1 = 1.0 / (1.0 + e2)
    g2 = e2 * g1
    meta = jnp.where(lane == 0, i1.astype(F32),
                     jnp.where(lane == 1, i2.astype(F32),
                               jnp.where(lane == 2, g1, jnp.where(lane == 3, g2, 0.0))))
    meta_ref[...] = meta


def _final_norm_body(x_ref, g_ref, o_ref):
    o_ref[...] = _rms(x_ref[...], g_ref[...])


def _mod_spec(rt, d, layer, chunk, dims):
    t_ctx, l_lat = dims
    return pl.BlockSpec(
        (1, 1, d), lambda i: (layer * MOD_ROWS + _mod_row(i * rt, t_ctx, l_lat), 0, chunk))


def _norm_mod(x, g_all, layer, modr, sh_chunk, sc_chunk, dims, rt):
    t, d = x.shape
    return pl.pallas_call(
        _norm_mod_body,
        grid=(t // rt,),
        in_specs=[pl.BlockSpec((rt, d), lambda i: (i, 0)),
                  pl.BlockSpec((1, 1, d), lambda i: (layer, 0, 0)),
                  _mod_spec(rt, d, layer, sh_chunk, dims),
                  _mod_spec(rt, d, layer, sc_chunk, dims)],
        out_specs=pl.BlockSpec((rt, d), lambda i: (i, 0)),
        out_shape=_sds((t, d), BF16),
        compiler_params=_cp(("arbitrary",)),
        name="norm_modulate",
    )(x, g_all.reshape(-1, 1, d), modr, modr)


def _norm_router(x, g_all, layer, modr, sh_chunk, sc_chunk, router_pad, n_exp, dims, rt):
    t, d = x.shape
    return pl.pallas_call(
        functools.partial(_norm_router_body, n_exp=n_exp),
        grid=(t // rt,),
        in_specs=[pl.BlockSpec((rt, d), lambda i: (i, 0)),
                  pl.BlockSpec((1, 1, d), lambda i: (layer, 0, 0)),
                  _mod_spec(rt, d, layer, sh_chunk, dims),
                  _mod_spec(rt, d, layer, sc_chunk, dims),
                  pl.BlockSpec((d, LANES), lambda i: (0, 0))],
        out_specs=[pl.BlockSpec((rt, d // 2), lambda i: (i, 0)),
                   pl.BlockSpec((rt, LANES), lambda i: (i, 0))],
        out_shape=[_sds((t, d // 2), jnp.uint32), _sds((t, LANES), F32)],
        compiler_params=_cp(("arbitrary",)),
        name="norm_router",
    )(x, g_all.reshape(-1, 1, d), modr, modr, router_pad)


def _final_norm(x, g, rt):
    t, d = x.shape
    return pl.pallas_call(
        _final_norm_body,
        grid=(t // rt,),
        in_specs=[pl.BlockSpec((rt, d), lambda i: (i, 0)),
                  pl.BlockSpec((1, d), lambda i: (0, 0))],
        out_specs=pl.BlockSpec((rt, d), lambda i: (i, 0)),
        out_shape=_sds((t, d), F32),
        compiler_params=_cp(("arbitrary",)),
        name="final_norm",
    )(x, g.reshape(1, d))


def _mm_body(*refs, nw, has_bias, epi, grouped, has_res):
    it = iter(refs)
    if grouped:
        te_ref = next(it)
        nv_ref = next(it)
        next(it)
    x_ref = next(it)
    w_refs = [next(it) for _ in range(nw)]
    b_refs = [next(it) for _ in range(nw)] if has_bias else []
    if has_res:
        res_ref = next(it)
        gate_ref = next(it)
    o_ref = next(it)
    wc_refs = [next(it) for _ in range(nw)]

    i = pl.program_id(1)
    if grouped:
        recast = (i == 0) | (te_ref[i] != te_ref[jnp.maximum(i - 1, 0)])
    else:
        recast = i == 0

    @pl.when(recast)
    def _():
        for w_ref, wc in zip(w_refs, wc_refs):
            wc[...] = w_ref[0].astype(BF16)

    tm = x_ref.shape[0]

    def compute(rows=tm):
        if x_ref.dtype == jnp.uint32:
            lo, hi = _unpack_halves(x_ref[0:rows, :])
            half = lo.shape[1]
            accs = [jnp.dot(lo, wc[0:half, :], preferred_element_type=F32)
                    + jnp.dot(hi, wc[half:2 * half, :], preferred_element_type=F32)
                    for wc in wc_refs]
        else:
            x = x_ref[0:rows, :].astype(BF16)
            accs = [jnp.dot(x, wc[...], preferred_element_type=F32) for wc in wc_refs]
        if has_bias:
            accs = [a + b[0] for a, b in zip(accs, b_refs)]
        if epi == "plain":
            y = accs[0]
        elif epi == "swiglu":
            y = accs[0] * jax.nn.sigmoid(accs[0]) * accs[1]
        elif epi == "glu":
            y = accs[0] * jax.nn.sigmoid(accs[1])
        if has_res:
            y = res_ref[...] + gate_ref[0] * y
        o_ref[0:rows, :] = y.astype(o_ref.dtype)
        if rows < tm:
            o_ref[rows:tm, :] = jnp.zeros((tm - rows, o_ref.shape[1]), o_ref.dtype)

    if grouped:
        nv = nv_ref[i]
        for c in range(1, ROW_CHUNKS + 1):
            pl.when(nv == c)(functools.partial(compute, c * (tm // ROW_CHUNKS)))

        @pl.when(nv == 0)
        def _():
            o_ref[...] = jnp.zeros(o_ref.shape, o_ref.dtype)
    else:
        compute()


def _matmul(x, ws, widx, col0s, n_out, *, tm, tn, epi="plain", biases=None, bidx=0,
            res=None, gate=None, out_dtype=F32, group=None, name="matmul"):
    m, kx = x.shape
    k = ws[0].shape[1]
    nw = len(ws)
    grouped = group is not None
    has_bias = biases is not None
    has_res = res is not None
    tn = _fit(n_out, tn)
    grid = (n_out // tn, m // tm)

    if grouped:
        te, nv, na, base = group

        def wmap(c0):
            return lambda n, i, te_r, nv_r, na_r: (base + te_r[i], 0, c0 // tn + n)

        def rmap(f):
            return lambda n, i, te_r, nv_r, na_r: f(n, i)

        xmap = lambda n, i, te_r, nv_r, na_r: (jnp.minimum(i, na_r[0] - 1), 0)
    else:
        def wmap(c0):
            return lambda n, i: (widx, 0, c0 // tn + n)

        def rmap(f):
            return f

        xmap = lambda n, i: (i, 0)

    in_specs = [pl.BlockSpec((tm, kx), xmap)]
    args = [x]
    for w, c0 in zip(ws, col0s):
        in_specs.append(pl.BlockSpec((1, k, tn), wmap(c0)))
        args.append(w)
    if has_bias:
        for b, c0 in zip(biases, col0s):
            in_specs.append(pl.BlockSpec((1, 1, tn), rmap(
                lambda n, i, c0=c0: (bidx, 0, c0 // tn + n))))
            args.append(b)
    if has_res:
        modr, layer, chunk, (t_ctx, l_lat) = gate
        in_specs.append(pl.BlockSpec((tm, tn), rmap(lambda n, i: (i, n))))
        args.append(res)
        in_specs.append(pl.BlockSpec((1, 1, tn), rmap(
            lambda n, i: (layer * MOD_ROWS + _mod_row(i * tm, t_ctx, l_lat), 0,
                          chunk * (n_out // tn) + n))))
        args.append(modr)
    out_spec = pl.BlockSpec((tm, tn), rmap(lambda n, i: (i, n)))
    scratch = [pltpu.VMEM((k, tn), BF16) for _ in range(nw)]
    body = functools.partial(_mm_body, nw=nw, has_bias=has_bias, epi=epi, grouped=grouped,
                             has_res=has_res)
    if grouped:
        grid_spec = pltpu.PrefetchScalarGridSpec(
            num_scalar_prefetch=3, grid=grid, in_specs=in_specs, out_specs=out_spec,
            scratch_shapes=scratch)
        return pl.pallas_call(body, grid_spec=grid_spec, out_shape=_sds((m, n_out), out_dtype),
                              compiler_params=_cp(("arbitrary", "arbitrary"), big=True),
                              name=name)(te, nv, na, *args)
    return pl.pallas_call(body, grid=grid, in_specs=in_specs, out_specs=out_spec,
                          out_shape=_sds((m, n_out), out_dtype), scratch_shapes=scratch,
                          compiler_params=_cp(("arbitrary", "arbitrary"), big=True),
                          name=name)(*args)


def _dft_basis(r, s, L):
    n = 2 * L
    f = jnp.where(r < L, r, r - L)
    k = (f * s + jnp.where(r < L, 0, n // 4)) & (n - 1)
    val = jnp.cos(k.astype(F32) * (2.0 * math.pi / n))
    nyq = (1 - 2 * (s & 1)).astype(F32)
    return jnp.where(r == L, nyq, val)


def _dft_body(f_ref, g_ref, *, L, rb):
    i = pl.program_id(0)
    r = i * rb + lax.broadcasted_iota(jnp.int32, (rb, L), 0)
    s = lax.broadcasted_iota(jnp.int32, (rb, L), 1)
    f_ref[...] = _dft_basis(r, s, L)
    t = lax.broadcasted_iota(jnp.int32, (L, rb), 0)
    r2 = i * rb + lax.broadcasted_iota(jnp.int32, (L, rb), 1)
    scale = jnp.where((r2 == 0) | (r2 == L), 1.0 / (2 * L), 1.0 / L)
    g_ref[...] = _dft_basis(r2, t, L) * scale


def _dft_tables(L):
    rb = min(256, L)
    return pl.pallas_call(
        functools.partial(_dft_body, L=L, rb=rb),
        grid=(2 * L // rb,),
        out_specs=[pl.BlockSpec((rb, L), lambda i: (i, 0)),
                   pl.BlockSpec((L, rb), lambda i: (0, i))],
        out_shape=[_sds((2 * L, L), F32), _sds((L, 2 * L), F32)],
        compiler_params=_cp(("arbitrary",)),
        name="dft_tables",
    )()


def _filter_body(z_ref, w1_ref, b1_ref, fr1_ref, w2_ref, b2_ref, fr2_ref, w3a_ref, w3b_ref,
                 dl_ref, f_ref, o_ref, *, L):
    z = z_ref[...]
    dot = functools.partial(jnp.dot, precision=HIGHEST, preferred_element_type=F32)
    h = jnp.sin(fr1_ref[...] * (dot(z, w1_ref[0]) + b1_ref[...]))
    h = jnp.sin(fr2_ref[...] * (dot(h, w2_ref[0]) + b2_ref[...]))
    t = z[:, 0:1]
    decay = jnp.exp(-t * dl_ref[...])
    hf = dot(h, w3a_ref[0]) * decay
    hb = dot(h, w3b_ref[0]) * decay
    row = lax.broadcasted_iota(jnp.int32, (L, 1), 0)
    hb = jnp.where(row == 0, 0.0, hb)
    nrm = (jnp.sum(jnp.abs(hf), axis=0, keepdims=True)
           + jnp.sum(jnp.abs(hb), axis=0, keepdims=True))
    p = dot(f_ref[0:L, :], hf + hb)
    q = dot(f_ref[L:2 * L, :], hf - hb)
    sign = (1 - 2 * (row & 1)).astype(F32)
    nyq_fix = 2.0 * jnp.sum(sign * hb, axis=0, keepdims=True)
    q = q + jnp.where(row == 0, nyq_fix, 0.0)
    inv = 1.0 / nrm
    o_ref[0, 0:L, :] = p * inv
    o_ref[0, L:2 * L, :] = q * inv


def _pad2(a, rows, cols):
    return jnp.pad(a, ((0, rows - a.shape[0]), (0, cols - a.shape[1])))


def _hyena_filters(L, li, hy_w1, hy_b1, hy_fr1, hy_w2, hy_b2, hy_fr2, hy_w3, f_tab, hy_ch):
    bands = (HY_EMB - 1) // 2
    pos = jnp.arange(L, dtype=F32)
    t = jnp.linspace(0.0, 1.0, L, dtype=F32)
    f = jnp.linspace(1e-4, bands - 1, bands, dtype=F32)
    ang = 2.0 * math.pi * pos[:, None] * f[None, :] / L
    z = jnp.concatenate([t[:, None], jnp.cos(ang), -jnp.sin(ang)], axis=-1)
    z = _pad2(z, L, LANES)
    max_decay = math.log(HY_TARGET) / HY_FAST_PCT
    min_decay = math.log(HY_TARGET) / HY_SLOW_PCT
    deltas = jnp.abs(jnp.linspace(min_decay, max_decay, hy_ch, dtype=F32)).reshape(1, hy_ch)
    n_even = hy_w1.shape[0]
    w1 = jnp.pad(hy_w1, ((0, 0), (0, LANES - HY_EMB), (0, LANES - HY_FILT_HID)))
    w2 = jnp.pad(hy_w2, ((0, 0), (0, LANES - HY_FILT_HID), (0, LANES - HY_FILT_HID)))
    w3 = jnp.pad(hy_w3, ((0, 0), (0, LANES - HY_FILT_HID), (0, 0)))
    padv = lambda a: _pad2(a[li:li + 1], 1, LANES)
    cb = min(256, hy_ch)
    ncb = hy_ch // cb
    vec = pl.BlockSpec((1, LANES), lambda j: (0, 0))
    sq = pl.BlockSpec((1, LANES, LANES), lambda j: (li, 0, 0))
    return pl.pallas_call(
        functools.partial(_filter_body, L=L),
        grid=(2 * ncb,),
        in_specs=[pl.BlockSpec((L, LANES), lambda j: (0, 0)),
                  sq, vec, vec, sq, vec, vec,
                  pl.BlockSpec((1, LANES, cb), lambda j: (li, 0, j)),
                  pl.BlockSpec((1, LANES, cb), lambda j: (li, 0, 2 * ncb + j)),
                  pl.BlockSpec((1, cb), lambda j: (0, j % ncb)),
                  pl.BlockSpec((2 * L, L), lambda j: (0, 0))],
        out_specs=pl.BlockSpec((1, 2 * L, cb), lambda j: (j // ncb, 0, j % ncb)),
        out_shape=_sds((2, 2 * L, hy_ch), F32),
        compiler_params=_cp(("arbitrary",), big=True),
        name="hyena_filters",
    )(z, w1, padv(hy_b1), padv(hy_fr1), w2, padv(hy_b2), padv(hy_fr2), w3, w3, deltas, f_tab)


def _hyena_body(ymix_ref, v_ref, x1_ref, x2_ref, swv_ref, sw1_ref, sw2_ref, sbv_ref, sb1_ref,
                sb2_ref, fb_ref, hs_ref, f_ref, g_ref, o_ref, *, L, spb):
    del ymix_ref
    row = lax.broadcasted_iota(jnp.int32, (L, 1), 0)

    def short(x_ref, w_ref, b_ref, r0):
        x = x_ref[r0:r0 + L, :]
        w = w_ref[0]
        xm = jnp.where(row == 0, 0.0, pltpu.roll(x, 1, 0))
        xp = jnp.where(row == L - 1, 0.0, pltpu.roll(x, L - 1, 0))
        return xm * w[0:1] + x * w[1:2] + xp * w[2:3] + b_ref[0]

    def fconv(z, order):
        zf = jnp.dot(f_ref[...], z.astype(BF16), preferred_element_type=F32)
        p, q = zf[0:L], zf[L:2 * L]
        hp, hq = hs_ref[order, 0:L, :], hs_ref[order, L:2 * L, :]
        yr = jnp.where(row == 0, p * hp, p * hp - q * hq)
        yq = jnp.where(row == 0, q * hq, p * hq + q * hp)
        return (jnp.dot(g_ref[:, 0:L], yr.astype(BF16), preferred_element_type=F32)
                + jnp.dot(g_ref[:, L:2 * L], yq.astype(BF16), preferred_element_type=F32))

    fb = fb_ref[0]
    for s in range(spb):
        r0 = s * L
        v = short(v_ref, swv_ref, sbv_ref, r0)
        x1 = short(x1_ref, sw1_ref, sb1_ref, r0)
        x2 = short(x2_ref, sw2_ref, sb2_ref, r0)
        z1 = x1 * (fconv(v, 0) + fb[0:1] * v)
        o_ref[r0:r0 + L, :] = (x2 * (fconv(z1, 1) + fb[1:2] * z1)).astype(o_ref.dtype)


def _hyena(proj, ymix, li, nseq, L, row_blk0, hy_short_w, hy_short_b, hy_fbias, hs, f_bf, g_bf,
           hy_ch):
    t, d = ymix.shape
    cb = min(256, hy_ch)
    ncb = hy_ch // cb
    spb = 2 if nseq % 2 == 0 and row_blk0 % 2 == 0 else 1
    rb0 = row_blk0 // spb
    rowmap = lambda c: (lambda j, b: (rb0 + b, c * ncb + j))
    wmap = lambda c: (lambda j, b: (li, 0, c * ncb + j))
    sb = hy_short_b.reshape(hy_short_b.shape[0], 1, -1)
    in_specs = [pl.BlockSpec(memory_space=pl.ANY),
                pl.BlockSpec((spb * L, cb), rowmap(0)), pl.BlockSpec((spb * L, cb), rowmap(1)),
                pl.BlockSpec((spb * L, cb), rowmap(2)),
                pl.BlockSpec((1, 3, cb), wmap(0)), pl.BlockSpec((1, 3, cb), wmap(1)),
                pl.BlockSpec((1, 3, cb), wmap(2)),
                pl.BlockSpec((1, 1, cb), wmap(0)), pl.BlockSpec((1, 1, cb), wmap(1)),
                pl.BlockSpec((1, 1, cb), wmap(2)),
                pl.BlockSpec((1, 2, cb), lambda j, b: (li, 0, j)),
                pl.BlockSpec((2, 2 * L, cb), lambda j, b: (0, 0, j)),
                pl.BlockSpec((2 * L, L), lambda j, b: (0, 0)),
                pl.BlockSpec((L, 2 * L), lambda j, b: (0, 0))]
    args = [ymix, proj, proj, proj, hy_short_w, hy_short_w, hy_short_w, sb, sb, sb, hy_fbias, hs,
            f_bf, g_bf]
    return pl.pallas_call(
        functools.partial(_hyena_body, L=L, spb=spb),
        grid=(ncb, nseq // spb),
        in_specs=in_specs,
        out_specs=pl.BlockSpec((spb * L, cb), lambda j, b: (rb0 + b, j)),
        out_shape=_sds((t, d), BF16),
        input_output_aliases={0: 0},
        compiler_params=_cp(("arbitrary", "arbitrary"), big=True),
        name="hyena_mixer",
    )(*args)


def _rope_tables(L):
    half = DA_DH // 2
    nf = half // 2
    lane = jnp.arange(LANES)
    sub = lane % DA_DH
    idx = sub % half
    inv = ROPE_THETA ** (-(idx % nf).astype(F32) / nf)
    pos = jnp.arange(L)
    p = jnp.where((sub // half)[None, :] == 0, (pos // GRID_W)[:, None], (pos % GRID_W)[:, None])
    ang = p.astype(F32) * inv[None, :]
    sgn = jnp.where(idx < nf, -1.0, 1.0)[None, :]
    return jnp.cos(ang), jnp.sin(ang) * sgn


def _attn_body(*refs, li, cached, tq):
    it = iter(refs)
    next(it)
    lq1, lk1, lq2, lk2 = [next(it) for _ in range(4)]
    q_ref, k_ref, v_ref = next(it), next(it), next(it)
    if cached:
        ck_ref, cv_ref, cq_ref, sq_ref, ckk_ref, skk_ref = [next(it) for _ in range(6)]
    g_ref = next(it)
    o_ref = next(it)

    lam = (jnp.exp(jnp.sum(lq1[...] * lk1[...], keepdims=True))
           - jnp.exp(jnp.sum(lq2[...] * lk2[...], keepdims=True)) + li)
    lane = lax.broadcasted_iota(jnp.int32, (1, LANES), 1)
    first = lane < DA_DH
    nf = DA_DH // 4

    def rope(x, c_ref, s_ref):
        swap = jnp.where((lane % (2 * nf)) < nf, pltpu.roll(x, LANES - nf, 1), pltpu.roll(x, nf, 1))
        return x * c_ref[...] + swap * s_ref[...]

    q = q_ref[...]
    k = k_ref[...]
    if cached:
        q = rope(q, cq_ref, sq_ref)
        k = rope(k, ckk_ref, skk_ref)
    q = q * (DA_DH ** -0.5)
    qs = [jnp.where(first, q, 0.0).astype(BF16), jnp.where(first, 0.0, q).astype(BF16)]
    nt = (((1,), (1,)), ((), ()))
    kb = k.astype(BF16)
    vb = v_ref[...].astype(BF16)
    if cached:
        ckb = ck_ref[0, 0].astype(BF16)
        cvb = cv_ref[0, 0].astype(BF16)

    def probs(qm):
        s = lax.dot_general(qm, kb, nt, preferred_element_type=F32)
        m = jnp.max(s, axis=-1, keepdims=True)
        if cached:
            sc = lax.dot_general(qm, ckb, nt, preferred_element_type=F32)
            m = jnp.maximum(m, jnp.max(sc, axis=-1, keepdims=True))
            ec = jnp.exp(sc - m)
        e = jnp.exp(s - m)
        den = jnp.sum(e, axis=-1, keepdims=True)
        if cached:
            den = den + jnp.sum(ec, axis=-1, keepdims=True)
            return e / den, ec / den
        return e / den, None

    p1, pc1 = probs(qs[0])
    p2, pc2 = probs(qs[1])
    o = jnp.dot((p1 - lam * p2).astype(BF16), vb, preferred_element_type=F32)
    if cached:
        o = o + jnp.dot((pc1 - lam * pc2).astype(BF16), cvb, preferred_element_type=F32)
    o = _rms(o, g_ref[0]) * (1.0 - li)
    o_ref[...] = o.astype(o_ref.dtype)


def _attention(proj, ymix, li, layer, nseq, L, row0, lams, da_norm_g, hy_ch, da_width,
               cache=None):
    t, d = ymix.shape
    heads = da_width // DA_VD
    cached = cache is not None
    tq = min(512, L)
    nq = L // tq
    qc0 = 3 * hy_ch // LANES
    kc0 = qc0 + da_width // LANES
    vc0 = kc0 + da_width // LANES
    lam_spec = pl.BlockSpec((1, DA_DH), lambda b, h, qi: (0, 0))
    in_specs = [pl.BlockSpec(memory_space=pl.ANY)] + [lam_spec] * 4 + [
        pl.BlockSpec((tq, LANES), lambda b, h, qi: (row0 // tq + b * nq + qi, qc0 + h)),
        pl.BlockSpec((L, LANES), lambda b, h, qi: (row0 // L + b, kc0 + h)),
        pl.BlockSpec((L, LANES), lambda b, h, qi: (row0 // L + b, vc0 + h))]
    args = [ymix] + [a[li:li + 1] for a in lams] + [proj, proj, proj]
    if cached:
        ck, cv, cos_t, sin_t = cache
        past = ck.shape[2]
        in_specs += [pl.BlockSpec((1, 1, past, LANES), lambda b, h, qi: (b, li, 0, h)),
                     pl.BlockSpec((1, 1, past, LANES), lambda b, h, qi: (b, li, 0, h)),
                     pl.BlockSpec((tq, LANES), lambda b, h, qi: (qi, 0)),
                     pl.BlockSpec((tq, LANES), lambda b, h, qi: (qi, 0)),
                     pl.BlockSpec((L, LANES), lambda b, h, qi: (0, 0)),
                     pl.BlockSpec((L, LANES), lambda b, h, qi: (0, 0))]
        args += [ck, cv, cos_t, sin_t, cos_t, sin_t]
    in_specs.append(pl.BlockSpec((1, LANES), lambda b, h, qi: (0, 0)))
    args.append(da_norm_g[li:li + 1])
    lam_init = 0.8 - 0.6 * math.exp(-0.3 * layer)
    return pl.pallas_call(
        functools.partial(_attn_body, li=lam_init, cached=cached, tq=tq),
        grid=(nseq, heads, nq),
        in_specs=in_specs,
        out_specs=pl.BlockSpec((tq, LANES),
                               lambda b, h, qi: (row0 // tq + b * nq + qi, hy_ch // LANES + h)),
        out_shape=_sds((t, d), BF16),
        input_output_aliases={0: 0},
        compiler_params=_cp(("arbitrary", "arbitrary", "arbitrary"), big=True),
        name="diff_attention",
    )(*args)


def _dw_tables_body(f_ref, g_ref, b_ref, *, rt, kp, taps):
    n = 2 * rt
    r = lax.broadcasted_iota(jnp.int32, (n, kp), 0)
    j = lax.broadcasted_iota(jnp.int32, (n, kp), 1)
    f_ref[...] = _dft_basis(r, j, rt)
    t = lax.broadcasted_iota(jnp.int32, (rt, n), 0)
    r2 = lax.broadcasted_iota(jnp.int32, (rt, n), 1)
    scale = jnp.where((r2 == 0) | (r2 == rt), 1.0 / n, 2.0 / n)
    shift = taps - 1 + HALO - taps // 2
    g_ref[...] = _dft_basis(r2, t + shift, rt) * scale
    r3 = lax.broadcasted_iota(jnp.int32, (n, LANES), 0)
    k3 = lax.broadcasted_iota(jnp.int32, (n, LANES), 1)
    b_ref[...] = _dft_basis(r3, (taps - 1 - k3) & (n - 1), rt)


def _dw_tables(rt, taps):
    kp = -(-(rt + 2 * HALO) // LANES) * LANES
    return pl.pallas_call(
        functools.partial(_dw_tables_body, rt=rt, kp=kp, taps=taps),
        out_shape=[_sds((2 * rt, kp), F32), _sds((rt, 2 * rt), F32), _sds((2 * rt, LANES), F32)],
        name="dwconv_tables",
    )()


def _dwconv_body(prev_ref, cur_ref, next_ref, w_ref, b_ref, lg_ref, lb_ref, f_ref, g_ref, bm_ref,
                 o_ref, buf, hd, *, rt, dims):
    t_ctx, l_ctx, l_lat = dims
    i = pl.program_id(0)

    @pl.when(i == 0)
    def _():
        hd[...] = jnp.dot(bm_ref[...], w_ref[0], precision=HIGHEST, preferred_element_type=F32)
        buf[rt + 2 * HALO:, :] = jnp.zeros((buf.shape[0] - rt - 2 * HALO, buf.shape[1]), F32)

    row0 = i * rt
    in_ctx = row0 < t_ctx
    seq_pos = jnp.where(in_ctx, row0 % l_ctx, (row0 - t_ctx) % l_lat)
    seq_len = jnp.where(in_ctx, l_ctx, l_lat)
    is_first = seq_pos == 0
    is_last = seq_pos + rt == seq_len
    buf[0:HALO, :] = jnp.where(is_first, 0.0, prev_ref[...])
    buf[HALO:HALO + rt, :] = cur_ref[...]
    buf[HALO + rt:2 * HALO + rt, :] = jnp.where(is_last, 0.0, next_ref[...])

    z = jnp.dot(f_ref[...], buf[...].astype(BF16), preferred_element_type=F32)
    p, q = z[0:rt], z[rt:2 * rt]
    hp, hq = hd[0:rt, :], hd[rt:2 * rt, :]
    row = lax.broadcasted_iota(jnp.int32, (rt, 1), 0)
    yr = jnp.where(row == 0, p * hp, p * hp - q * hq)
    yq = jnp.where(row == 0, q * hq, p * hq + q * hp)
    u = (jnp.dot(g_ref[:, 0:rt], yr.astype(BF16), preferred_element_type=F32)
         + jnp.dot(g_ref[:, rt:2 * rt], yq.astype(BF16), preferred_element_type=F32) + b_ref[0])
    mu = jnp.mean(u, axis=-1, keepdims=True)
    uc = u - mu
    var = jnp.mean(uc * uc, axis=-1, keepdims=True)
    y = uc * lax.rsqrt(var + EPS) * lg_ref[0] + lb_ref[0]
    o_ref[...] = (y * jax.nn.sigmoid(y)).astype(o_ref.dtype)


def _conformer_dwconv(u, li, cv_dw, cv_dwb, cv_ln_g, cv_ln_b, tabs, dims, rt):
    t, d = u.shape
    f_bf, g_bf, bm = tabs
    kp = f_bf.shape[1]
    hb = rt // HALO
    nhb = t // HALO
    w_pad = jnp.pad(cv_dw, ((0, 0), (0, LANES - cv_dw.shape[1]), (0, 0)))
    vec = lambda a: a.reshape(a.shape[0], 1, d)
    vspec = pl.BlockSpec((1, 1, d), lambda i: (li, 0, 0))
    full = lambda a: pl.BlockSpec(a.shape, lambda i: (0, 0))
    return pl.pallas_call(
        functools.partial(_dwconv_body, rt=rt, dims=dims),
        grid=(t // rt,),
        in_specs=[pl.BlockSpec((HALO, d), lambda i: (jnp.maximum(i * hb - 1, 0), 0)),
                  pl.BlockSpec((rt, d), lambda i: (i, 0)),
                  pl.BlockSpec((HALO, d), lambda i: (jnp.minimum((i + 1) * hb, nhb - 1), 0)),
                  pl.BlockSpec((1, LANES, d), lambda i: (li, 0, 0)),
                  vspec, vspec, vspec, full(f_bf), full(g_bf), full(bm)],
        out_specs=pl.BlockSpec((rt, d), lambda i: (i, 0)),
        out_shape=_sds((t, d), BF16),
        scratch_shapes=[pltpu.VMEM((kp, d), F32), pltpu.VMEM((2 * rt, d), F32)],
        compiler_params=_cp(("arbitrary",), big=True),
        name="conformer_dwconv",
    )(u, u, u, w_pad, vec(cv_dwb), vec(cv_ln_g), vec(cv_ln_b), f_bf, g_bf, bm)


def _scatter_body(pos_ref, pad_ref, h_ref, xs_ref, zrow, sem, zsem, *, rt, n_exp):
    i = pl.program_id(0)

    def issue(r, carry):
        for kk in range(TOP_K):
            p = pos_ref[(i * rt + r) * TOP_K + kk]
            pltpu.make_async_copy(h_ref.at[pl.ds(r, 1)], xs_ref.at[pl.ds(p, 1)], sem).start()
        return carry

    lax.fori_loop(0, rt, issue, 0, unroll=DMA_UNROLL)

    @pl.when(i == 0)
    def _():
        zrow[...] = jnp.zeros(zrow.shape, zrow.dtype)
        for e in range(n_exp):
            lo, hi = pad_ref[e], pad_ref[n_exp + e]

            def zissue(r, carry):
                pltpu.make_async_copy(zrow, xs_ref.at[pl.ds(r, 1)], zsem).start()
                return carry

            def zwait(r, carry):
                pltpu.make_async_copy(zrow, xs_ref.at[pl.ds(r, 1)], zsem).wait()
                return carry

            lax.fori_loop(lo, hi, zissue, 0)
            lax.fori_loop(lo, hi, zwait, 0)

    def drain(r, carry):
        for kk in range(TOP_K):
            pltpu.make_async_copy(h_ref.at[pl.ds(0, 1)], xs_ref.at[pl.ds(0, 1)], sem).wait()
        return carry

    lax.fori_loop(0, rt, drain, 0, unroll=DMA_UNROLL)


def _moe_scatter(h, pos, pad_ranges, n_rows, n_exp, rt):
    t, d = h.shape
    grid_spec = pltpu.PrefetchScalarGridSpec(
        num_scalar_prefetch=2, grid=(t // rt,),
        in_specs=[pl.BlockSpec((rt, d), lambda i, p, z: (i, 0))],
        out_specs=pl.BlockSpec(memory_space=pl.ANY),
        scratch_shapes=[pltpu.VMEM((1, d), h.dtype), pltpu.SemaphoreType.DMA,
                        pltpu.SemaphoreType.DMA])
    return pl.pallas_call(
        functools.partial(_scatter_body, rt=rt, n_exp=n_exp),
        grid_spec=grid_spec, out_shape=_sds((n_rows, d), h.dtype),
        compiler_params=_cp(("arbitrary",)),
        name="moe_scatter",
    )(pos, pad_ranges, h)


def _combine_body(pos_ref, x_ref, meta_ref, gate_ref, ys_ref, o_ref, buf, sem, *, rt):
    i = pl.program_id(0)

    def issue(r, carry):
        for kk in range(TOP_K):
            p = pos_ref[(i * rt + r) * TOP_K + kk]
            pltpu.make_async_copy(ys_ref.at[pl.ds(p, 1)], buf.at[kk, pl.ds(r, 1)], sem).start()
        return carry

    lax.fori_loop(0, rt, issue, 0, unroll=DMA_UNROLL)

    def drain(r, carry):
        for kk in range(TOP_K):
            pltpu.make_async_copy(ys_ref.at[pl.ds(0, 1)], buf.at[kk, pl.ds(0, 1)], sem).wait()
        return carry

    lax.fori_loop(0, rt, drain, 0, unroll=DMA_UNROLL)
    meta = meta_ref[...]
    f = meta[:, 2:3] * buf[0] + meta[:, 3:4] * buf[1]
    o_ref[...] = x_ref[...] + gate_ref[0] * f


def _moe_combine(x, ys, pos, meta, modr, layer, chunk, dims, rt):
    t, d = x.shape
    t_ctx, l_lat = dims
    grid_spec = pltpu.PrefetchScalarGridSpec(
        num_scalar_prefetch=1, grid=(t // rt,),
        in_specs=[pl.BlockSpec((rt, d), lambda i, p: (i, 0)),
                  pl.BlockSpec((rt, LANES), lambda i, p: (i, 0)),
                  pl.BlockSpec((1, 1, d), lambda i, p: (
                      layer * MOD_ROWS + _mod_row(i * rt, t_ctx, l_lat), 0, chunk)),
                  pl.BlockSpec(memory_space=pl.ANY)],
        out_specs=pl.BlockSpec((rt, d), lambda i, p: (i, 0)),
        scratch_shapes=[pltpu.VMEM((TOP_K, rt, d), F32), pltpu.SemaphoreType.DMA])
    return pl.pallas_call(
        functools.partial(_combine_body, rt=rt),
        grid_spec=grid_spec, out_shape=_sds((t, d), F32),
        compiler_params=_cp(("arbitrary",)),
        name="moe_combine",
    )(pos, x, meta, modr, ys)


def _routing_tables(meta, n_exp, tg, n_rows):
    ids = meta[:, 0:TOP_K].astype(jnp.int32).reshape(-1)
    onehot = (ids[:, None] == jnp.arange(n_exp)[None, :]).astype(jnp.int32)
    csum = jnp.cumsum(onehot, axis=0)
    counts = csum[-1]
    rank = jnp.sum(onehot * csum, axis=1) - 1
    padded = (counts + tg - 1) // tg * tg
    ends = jnp.cumsum(padded)
    starts = ends - padded
    pos = (starts[ids] + rank).astype(jnp.int32)
    hi = ends.at[-1].set(n_rows)
    pad_ranges = jnp.concatenate([starts + counts, hi]).astype(jnp.int32)
    return pos, pad_ranges, (starts + counts, ends)


def _tile_tables(geometry, tm, n_rows, base):
    real_end, ends = geometry
    tile_start = jnp.arange(n_rows // tm, dtype=jnp.int32) * tm
    te = jnp.sum((tile_start[:, None] >= ends[None, :]).astype(jnp.int32), axis=1)
    te_last = jnp.sum((jnp.maximum(ends[-1] - tm, 0) >= ends).astype(jnp.int32))
    active = tile_start < ends[-1]
    te = jnp.where(active, te, te_last).astype(jnp.int32)
    chunk = tm // ROW_CHUNKS
    real = jnp.clip(real_end[te] - tile_start, 0, tm)
    nv = jnp.where(active, (real + chunk - 1) // chunk, 0).astype(jnp.int32)
    n_active = (ends[-1] // tm).astype(jnp.int32).reshape(1)
    return te, nv, n_active, base


def kernel(x_prompt, x_sample, cache_k, cache_v, c, c_ctx, norm_mix_g, norm_ffn_g, w_mod, b_mod,
           w_in, w_out, hy_short_w, hy_short_b, hy_w1, hy_b1, hy_fr1, hy_w2, hy_b2, hy_fr2, hy_w3,
           hy_fbias, da_lq1, da_lk1, da_lq2, da_lk2, da_norm_g, cv_w1, cv_b1, cv_dw, cv_dwb,
           cv_ln_g, cv_ln_b, cv_w2, cv_b2, ff_w1, ff_w3, ff_w2, moe_router, moe_w1, moe_w3, moe_w2,
           final_g):
    b_ctx, l_ctx, d = x_prompt.shape
    b_lat, l_lat, _ = x_sample.shape
    t_ctx, t_lat = b_ctx * l_ctx, b_lat * l_lat
    t = t_ctx + t_lat
    depth = w_mod.shape[0]
    hy_ch = hy_fbias.shape[-1]
    da_width = d - hy_ch
    heads = da_width // DA_VD
    n_exp = moe_router.shape[-1]
    d_ff = ff_w1.shape[-1]
    e_ff = moe_w1.shape[-1]
    past = cache_k.shape[2]
    n_even = w_in.shape[0]
    dims = (t_ctx, l_lat)
    rt = min(256, l_ctx)
    tm_big = min(1024, l_lat, t_ctx)
    tm_res = min(512, l_lat, t_ctx)
    tm_up = min(1024, t)
    tm_down = min(512, t)
    assert b_lat + 1 <= MOD_ROWS and t_ctx % tm_big == 0 and l_lat % tm_big == 0

    x = jnp.concatenate([x_prompt.reshape(t_ctx, d), x_sample.reshape(t_lat, d)], axis=0)
    cond = jnp.zeros((MOD_ROWS, d), F32).at[0].set(c_ctx).at[1:1 + b_lat].set(c)
    modr = _mods(cond, w_mod, b_mod).reshape(depth * MOD_ROWS, 1, 6 * d)

    tables = {}
    for L in sorted({l_ctx, l_lat}):
        f_tab, g_tab = _dft_tables(L)
        tables[L] = (f_tab, f_tab.astype(BF16), g_tab.astype(BF16))
    dw_f, dw_g, dw_b = _dw_tables(rt, cv_dw.shape[1])
    dw_tabs = (dw_f.astype(BF16), dw_g.astype(BF16), dw_b)
    cos_t, sin_t = _rope_tables(l_lat)
    ck = cache_k.reshape(b_lat, n_even, past, heads * DA_VD)
    cv = cache_v.reshape(b_lat, n_even, past, heads * DA_VD)
    lams = (da_lq1, da_lk1, da_lq2, da_lk2)
    router_pad = jnp.pad(moe_router, ((0, 0), (0, 0), (0, LANES - n_exp)))
    moe_w1r = moe_w1.reshape(-1, d, e_ff)
    moe_w3r = moe_w3.reshape(-1, d, e_ff)
    moe_w2r = moe_w2.reshape(-1, e_ff, d)
    n_rows = ((t * TOP_K) // tm_up + n_exp) * tm_up
    vec3 = lambda a: a.reshape(a.shape[0], 1, a.shape[-1])

    ks, vs = [], []
    for layer in range(depth):
        li = layer // 2
        if layer % 2 == 0:
            h = _norm_mod(x, norm_mix_g, layer, modr, 0, 1, dims, tm_res)
            proj = _matmul(h, [w_in], li, [0], w_in.shape[-1], tm=tm_big,
                           tn=min(1024, w_in.shape[-1]), name="in_proj")
            ymix = jnp.zeros((t, d), BF16)
            for (nseq, L, row0) in ((b_ctx, l_ctx, 0), (b_lat, l_lat, t_ctx)):
                f_tab, f_bf, g_bf = tables[L]
                hs = _hyena_filters(L, li, hy_w1, hy_b1, hy_fr1, hy_w2, hy_b2, hy_fr2, hy_w3,
                                    f_tab, hy_ch)
                ymix = _hyena(proj, ymix, li, nseq, L, row0 // L, hy_short_w, hy_short_b,
                              hy_fbias, hs, f_bf, g_bf, hy_ch)
            ymix = _attention(proj, ymix, li, layer, b_ctx, l_ctx, 0, lams, da_norm_g, hy_ch,
                              da_width)
            ymix = _attention(proj, ymix, li, layer, b_lat, l_lat, t_ctx, lams, da_norm_g, hy_ch,
                              da_width, cache=(ck, cv, cos_t, sin_t))
            kc0 = 3 * hy_ch + da_width
            ks.append(proj[:t_ctx, kc0:kc0 + da_width].reshape(b_ctx, l_ctx, heads, DA_VD))
            vs.append(proj[:t_ctx, kc0 + da_width:kc0 + 2 * da_width]
                      .reshape(b_ctx, l_ctx, heads, DA_VD))
            x = _matmul(ymix, [w_out], li, [0], d, tm=tm_big, tn=min(1024, d), epi="plain",
                        res=x, gate=(modr, layer, 2, dims), name="out_proj")
            h = _norm_mod(x, norm_ffn_g, layer, modr, 3, 4, dims, tm_res)
            hid = _matmul(h, [ff_w1, ff_w3], li, [0, 0], d_ff, tm=tm_big, tn=min(512, d_ff),
                          epi="swiglu", out_dtype=BF16, name="ffn_up")
            x = _matmul(hid, [ff_w2], li, [0], d, tm=tm_res, tn=min(512, d), res=x,
                        gate=(modr, layer, 5, dims), name="ffn_down")
        else:
            h = _norm_mod(x, norm_mix_g, layer, modr, 0, 1, dims, tm_res)
            u = _matmul(h, [cv_w1, cv_w1], li, [0, d], d, tm=tm_big, tn=min(512, d), epi="glu",
                        biases=[vec3(cv_b1), vec3(cv_b1)], bidx=li, name="conformer_up")
            u = _conformer_dwconv(u, li, cv_dw, cv_dwb, cv_ln_g, cv_ln_b, dw_tabs,
                                  (t_ctx, l_ctx, l_lat), rt)
            x = _matmul(u, [cv_w2], li, [0], d, tm=tm_big, tn=min(1024, d), biases=[vec3(cv_b2)],
                        bidx=li, res=x, gate=(modr, layer, 2, dims), name="conformer_down")
            hpk, meta = _norm_router(x, norm_ffn_g, layer, modr, 3, 4, router_pad[li], n_exp,
                                     dims, tm_res)
            pos, pad_ranges, geometry = _routing_tables(meta, n_exp, tm_up, n_rows)
            xs = _moe_scatter(hpk, pos, pad_ranges, n_rows, n_exp, rt)
            group = _tile_tables(geometry, tm_up, n_rows, li * n_exp)
            hid = _matmul(xs, [moe_w1r, moe_w3r], 0, [0, 0], e_ff, tm=tm_up,
                          tn=min(512, e_ff), epi="swiglu", out_dtype=BF16, group=group,
                          name="moe_up")
            ys = _matmul(hid, [moe_w2r], 0, [0], d, tm=tm_down, tn=min(512, d),
                         group=_tile_tables(geometry, tm_down, n_rows, li * n_exp),
                         name="moe_down")
            x = _moe_combine(x, ys, pos, meta, modr, layer, 5, dims, rt)

    y = _final_norm(x, final_g, tm_res)
    y_prompt = y[:t_ctx].reshape(b_ctx, l_ctx, d)
    y_sample = y[t_ctx:].reshape(b_lat, l_lat, d)
    return (y_prompt, y_sample, jnp.stack(ks, axis=1), jnp.stack(vs, axis=1))
```

```python
import functools
import math

import jax
import jax.numpy as jnp
from jax import lax
from jax.experimental import pallas as pl
from jax.experimental.pallas import tpu as pltpu

F32 = jnp.float32
BF16 = jnp.bfloat16
HIGHEST = lax.Precision.HIGHEST

EPS = 1e-6
GRID_W = 64
HY_EMB = 33
HY_FILT_HID = 64
HY_FAST_PCT = 0.3
HY_SLOW_PCT = 1.5
HY_TARGET = 1e-2
DA_DH = 64
DA_VD = 2 * DA_DH
ROPE_THETA = 10000.0
TOP_K = 2
MOD_ROWS = 16
LANES = 128
V7X_VMEM_LIMIT = 56 * 2**20
HALO = 16
DMA_UNROLL = 8
ROW_CHUNKS = 4


def _cp(sem, big=False):
    return pltpu.CompilerParams(dimension_semantics=sem,
                                vmem_limit_bytes=V7X_VMEM_LIMIT if big else None)


def _sds(shape, dtype):
    return jax.ShapeDtypeStruct(shape, dtype)


def _fit(n, pref):
    best = LANES
    for w in range(LANES, min(n, pref) + 1, LANES):
        if n % w == 0:
            best = w
    return best


def _mod_row(row0, t_ctx, l_lat):
    return jnp.where(row0 < t_ctx, 0, 1 + (row0 - t_ctx) // l_lat)


def _mod_body(c_ref, w_ref, b_ref, o_ref):
    c = c_ref[...]
    s = (c * jax.nn.sigmoid(c)).astype(BF16)
    o_ref[0] = jnp.dot(s, w_ref[0].astype(BF16), preferred_element_type=F32) + b_ref[0]


def _mods(cond, w_mod, b_mod):
    depth, d, nm = w_mod.shape
    tn = _fit(nm, 1024)
    return pl.pallas_call(
        _mod_body,
        grid=(depth, nm // tn),
        in_specs=[pl.BlockSpec((MOD_ROWS, d), lambda l, n: (0, 0)),
                  pl.BlockSpec((1, d, tn), lambda l, n: (l, 0, n)),
                  pl.BlockSpec((1, 1, tn), lambda l, n: (l, 0, n))],
        out_specs=pl.BlockSpec((1, MOD_ROWS, tn), lambda l, n: (l, 0, n)),
        out_shape=_sds((depth, MOD_ROWS, nm), F32),
        compiler_params=_cp(("arbitrary", "arbitrary"), big=True),
        name="modulation",
    )(cond, w_mod, b_mod.reshape(depth, 1, nm))


def _rms(x, g):
    return x * lax.rsqrt(jnp.mean(x * x, axis=-1, keepdims=True) + EPS) * g


def _norm_mod_body(x_ref, g_ref, sh_ref, sc_ref, o_ref):
    y = _rms(x_ref[...], g_ref[0])
    o_ref[...] = (y * (1.0 + sc_ref[0]) + sh_ref[0]).astype(o_ref.dtype)


def _pack_halves(h):
    half = h.shape[1] // 2
    bits = lax.bitcast_convert_type(h.astype(BF16).astype(F32), jnp.uint32)
    return (bits[:, :half] >> 16) | (bits[:, half:] & jnp.uint32(0xFFFF0000))


def _unpack_halves(words):
    lo = lax.bitcast_convert_type(words << 16, F32).astype(BF16)
    hi = lax.bitcast_convert_type(words & jnp.uint32(0xFFFF0000), F32).astype(BF16)
    return lo, hi


def _split_bf16(a):
    hi = a.astype(BF16)
    return hi, (a - hi.astype(F32)).astype(BF16)


def _dot3(a, b):
    ah, al = _split_bf16(a)
    bh, bl = _split_bf16(b)
    dot = functools.partial(jnp.dot, preferred_element_type=F32)
    return dot(ah, bh) + (dot(ah, bl) + dot(al, bh))


def _norm_router_body(x_ref, g_ref, sh_ref, sc_ref, r_ref, h_ref, meta_ref, *, n_exp):
    y = _rms(x_ref[...], g_ref[0])
    h = y * (1.0 + sc_ref[0]) + sh_ref[0]
    h_ref[...] = _pack_halves(h)
    logits = _dot3(h, r_ref[...])
    lane = lax.broadcasted_iota(jnp.int32, logits.shape, 1)
    neg = jnp.float32(-jnp.inf)
    logits = jnp.where(lane < n_exp, logits, neg)
    m1 = jnp.max(logits, axis=-1, keepdims=True)
    i1 = jnp.min(jnp.where(logits == m1, lane, LANES), axis=-1, keepdims=True)
    rest = jnp.where(lane == i1, neg, logits)
    m2 = jnp.max(rest, axis=-1, keepdims=True)
    i2 = jnp.min(jnp.where(rest == m2, lane, LANES), axis=-1, keepdims=True)
    e2 = jnp.exp(m2 - m1)
    g1 = 1.0 / (1.0 + e2)
    g2 = e2 * g1
    meta = jnp.where(lane == 0, i1.astype(F32),
                     jnp.where(lane == 1, i2.astype(F32),
                               jnp.where(lane == 2, g1, jnp.where(lane == 3, g2, 0.0))))
    meta_ref[...] = meta


def _final_norm_body(x_ref, g_ref, o_ref):
    o_ref[...] = _rms(x_ref[...], g_ref[...])


def _mod_spec(rt, d, layer, chunk, dims):
    t_ctx, l_lat = dims
    return pl.BlockSpec(
        (1, 1, d), lambda i: (layer * MOD_ROWS + _mod_row(i * rt, t_ctx, l_lat), 0, chunk))


def _norm_mod(x, g_all, layer, modr, sh_chunk, sc_chunk, dims, rt):
    t, d = x.shape
    return pl.pallas_call(
        _norm_mod_body,
        grid=(t // rt,),
        in_specs=[pl.BlockSpec((rt, d), lambda i: (i, 0)),
                  pl.BlockSpec((1, 1, d), lambda i: (layer, 0, 0)),
                  _mod_spec(rt, d, layer, sh_chunk, dims),
                  _mod_spec(rt, d, layer, sc_chunk, dims)],
        out_specs=pl.BlockSpec((rt, d), lambda i: (i, 0)),
        out_shape=_sds((t, d), BF16),
        compiler_params=_cp(("arbitrary",)),
        name="norm_modulate",
    )(x, g_all.reshape(-1, 1, d), modr, modr)


def _norm_router(x, g_all, layer, modr, sh_chunk, sc_chunk, router_pad, n_exp, dims, rt):
    t, d = x.shape
    return pl.pallas_call(
        functools.partial(_norm_router_body, n_exp=n_exp),
        grid=(t // rt,),
        in_specs=[pl.BlockSpec((rt, d), lambda i: (i, 0)),
                  pl.BlockSpec((1, 1, d), lambda i: (layer, 0, 0)),
                  _mod_spec(rt, d, layer, sh_chunk, dims),
                  _mod_spec(rt, d, layer, sc_chunk, dims),
                  pl.BlockSpec((d, LANES), lambda i: (0, 0))],
        out_specs=[pl.BlockSpec((rt, d // 2), lambda i: (i, 0)),
                   pl.BlockSpec((rt, LANES), lambda i: (i, 0))],
        out_shape=[_sds((t, d // 2), jnp.uint32), _sds((t, LANES), F32)],
        compiler_params=_cp(("arbitrary",)),
        name="norm_router",
    )(x, g_all.reshape(-1, 1, d), modr, modr, router_pad)


def _final_norm(x, g, rt):
    t, d = x.shape
    return pl.pallas_call(
        _final_norm_body,
        grid=(t // rt,),
        in_specs=[pl.BlockSpec((rt, d), lambda i: (i, 0)),
                  pl.BlockSpec((1, d), lambda i: (0, 0))],
        out_specs=pl.BlockSpec((rt, d), lambda i: (i, 0)),
        out_shape=_sds((t, d), F32),
        compiler_params=_cp(("arbitrary",)),
        name="final_norm",
    )(x, g.reshape(1, d))


def _mm_body(*refs, nw, has_bias, epi, grouped, has_res):
    it = iter(refs)
    if grouped:
        te_ref = next(it)
        nv_ref = next(it)
        next(it)
    x_ref = next(it)
    w_refs = [next(it) for _ in range(nw)]
    b_refs = [next(it) for _ in range(nw)] if has_bias else []
    if has_res:
        res_ref = next(it)
        gate_ref = next(it)
    o_ref = next(it)
    wc_refs = [next(it) for _ in range(nw)]

    i = pl.program_id(1)
    if grouped:
        recast = (i == 0) | (te_ref[i] != te_ref[jnp.maximum(i - 1, 0)])
    else:
        recast = i == 0

    @pl.when(recast)
    def _():
        for w_ref, wc in zip(w_refs, wc_refs):
            wc[...] = w_ref[0].astype(BF16)

    tm = x_ref.shape[0]

    def compute(rows=tm):
        if x_ref.dtype == jnp.uint32:
            lo, hi = _unpack_halves(x_ref[0:rows, :])
            half = lo.shape[1]
            accs = [jnp.dot(lo, wc[0:half, :], preferred_element_type=F32)
                    + jnp.dot(hi, wc[half:2 * half, :], preferred_element_type=F32)
                    for wc in wc_refs]
        else:
            x = x_ref[0:rows, :].astype(BF16)
            accs = [jnp.dot(x, wc[...], preferred_element_type=F32) for wc in wc_refs]
        if has_bias:
            accs = [a + b[0] for a, b in zip(accs, b_refs)]
        if epi == "plain":
            y = accs[0]
        elif epi == "swiglu":
            y = accs[0] * jax.nn.sigmoid(accs[0]) * accs[1]
        elif epi == "glu":
            y = accs[0] * jax.nn.sigmoid(accs[1])
        if has_res:
            y = res_ref[...] + gate_ref[0] * y
        o_ref[0:rows, :] = y.astype(o_ref.dtype)
        if rows < tm:
            o_ref[rows:tm, :] = jnp.zeros((tm - rows, o_ref.shape[1]), o_ref.dtype)

    if grouped:
        nv = nv_ref[i]
        for c in range(1, ROW_CHUNKS + 1):
            pl.when(nv == c)(functools.partial(compute, c * (tm // ROW_CHUNKS)))

        @pl.when(nv == 0)
        def _():
            o_ref[...] = jnp.zeros(o_ref.shape, o_ref.dtype)
    else:
        compute()


def _matmul(x, ws, widx, col0s, n_out, *, tm, tn, epi="plain", biases=None, bidx=0,
            res=None, gate=None, out_dtype=F32, group=None, name="matmul"):
    m, kx = x.shape
    k = ws[0].shape[1]
    nw = len(ws)
    grouped = group is not None
    has_bias = biases is not None
    has_res = res is not None
    tn = _fit(n_out, tn)
    grid = (n_out // tn, m // tm)

    if grouped:
        te, nv, na, base = group

        def wmap(c0):
            return lambda n, i, te_r, nv_r, na_r: (base + te_r[i], 0, c0 // tn + n)

        def rmap(f):
            return lambda n, i, te_r, nv_r, na_r: f(n, i)

        xmap = lambda n, i, te_r, nv_r, na_r: (jnp.minimum(i, na_r[0] - 1), 0)
    else:
        def wmap(c0):
            return lambda n, i: (widx, 0, c0 // tn + n)

        def rmap(f):
            return f

        xmap = lambda n, i: (i, 0)

    in_specs = [pl.BlockSpec((tm, kx), xmap)]
    args = [x]
    for w, c0 in zip(ws, col0s):
        in_specs.append(pl.BlockSpec((1, k, tn), wmap(c0)))
        args.append(w)
    if has_bias:
        for b, c0 in zip(biases, col0s):
            in_specs.append(pl.BlockSpec((1, 1, tn), rmap(
                lambda n, i, c0=c0: (bidx, 0, c0 // tn + n))))
            args.append(b)
    if has_res:
        modr, layer, chunk, (t_ctx, l_lat) = gate
        in_specs.append(pl.BlockSpec((tm, tn), rmap(lambda n, i: (i, n))))
        args.append(res)
        in_specs.append(pl.BlockSpec((1, 1, tn), rmap(
            lambda n, i: (layer * MOD_ROWS + _mod_row(i * tm, t_ctx, l_lat), 0,
                          chunk * (n_out // tn) + n))))
        args.append(modr)
    out_spec = pl.BlockSpec((tm, tn), rmap(lambda n, i: (i, n)))
    scratch = [pltpu.VMEM((k, tn), BF16) for _ in range(nw)]
    body = functools.partial(_mm_body, nw=nw, has_bias=has_bias, epi=epi, grouped=grouped,
                             has_res=has_res)
    if grouped:
        grid_spec = pltpu.PrefetchScalarGridSpec(
            num_scalar_prefetch=3, grid=grid, in_specs=in_specs, out_specs=out_spec,
            scratch_shapes=scratch)
        return pl.pallas_call(body, grid_spec=grid_spec, out_shape=_sds((m, n_out), out_dtype),
                              compiler_params=_cp(("arbitrary", "arbitrary"), big=True),
                              name=name)(te, nv, na, *args)
    return pl.pallas_call(body, grid=grid, in_specs=in_specs, out_specs=out_spec,
                          out_shape=_sds((m, n_out), out_dtype), scratch_shapes=scratch,
                          compiler_params=_cp(("arbitrary", "arbitrary"), big=True),
                          name=name)(*args)


def _dft_basis(r, s, L):
    n = 2 * L
    f = jnp.where(r < L, r, r - L)
    k = (f * s + jnp.where(r < L, 0, n // 4)) & (n - 1)
    val = jnp.cos(k.astype(F32) * (2.0 * math.pi / n))
    nyq = (1 - 2 * (s & 1)).astype(F32)
    return jnp.where(r == L, nyq, val)


def _dft_body(f_ref, g_ref, *, L, rb):
    i = pl.program_id(0)
    r = i * rb + lax.broadcasted_iota(jnp.int32, (rb, L), 0)
    s = lax.broadcasted_iota(jnp.int32, (rb, L), 1)
    f_ref[...] = _dft_basis(r, s, L)
    t = lax.broadcasted_iota(jnp.int32, (L, rb), 0)
    r2 = i * rb + lax.broadcasted_iota(jnp.int32, (L, rb), 1)
    scale = jnp.where((r2 == 0) | (r2 == L), 1.0 / (2 * L), 1.0 / L)
    g_ref[...] = _dft_basis(r2, t, L) * scale


def _dft_tables(L):
    rb = min(256, L)
    return pl.pallas_call(
        functools.partial(_dft_body, L=L, rb=rb),
        grid=(2 * L // rb,),
        out_specs=[pl.BlockSpec((rb, L), lambda i: (i, 0)),
                   pl.BlockSpec((L, rb), lambda i: (0, i))],
        out_shape=[_sds((2 * L, L), F32), _sds((L, 2 * L), F32)],
        compiler_params=_cp(("arbitrary",)),
        name="dft_tables",
    )()


def _filter_body(z_ref, w1_ref, b1_ref, fr1_ref, w2_ref, b2_ref, fr2_ref, w3a_ref, w3b_ref,
                 dl_ref, f_ref, o_ref, h_scr, *, L):
    dot = functools.partial(jnp.dot, precision=HIGHEST, preferred_element_type=F32)

    @pl.when(pl.program_id(0) == 0)
    def _():
        h1 = jnp.sin(fr1_ref[...] * (dot(z_ref[...], w1_ref[0]) + b1_ref[...]))
        h_scr[...] = jnp.sin(fr2_ref[...] * (dot(h1, w2_ref[0]) + b2_ref[...]))

    h = h_scr[...]
    t = z_ref[:, 0:1]
    decay = jnp.exp(-t * dl_ref[...])
    hf = dot(h, w3a_ref[0]) * decay
    hb = dot(h, w3b_ref[0]) * decay
    row = lax.broadcasted_iota(jnp.int32, (L, 1), 0)
    hb = jnp.where(row == 0, 0.0, hb)
    nrm = (jnp.sum(jnp.abs(hf), axis=0, keepdims=True)
           + jnp.sum(jnp.abs(hb), axis=0, keepdims=True))
    p = _dot3(f_ref[0:L, :], hf + hb)
    q = _dot3(f_ref[L:2 * L, :], hf - hb)
    sign = (1 - 2 * (row & 1)).astype(F32)
    nyq_fix = 2.0 * jnp.sum(sign * hb, axis=0, keepdims=True)
    q = q + jnp.where(row == 0, nyq_fix, 0.0)
    inv = 1.0 / nrm
    o_ref[0, 0:L, :] = p * inv
    o_ref[0, L:2 * L, :] = q * inv


def _pad2(a, rows, cols):
    return jnp.pad(a, ((0, rows - a.shape[0]), (0, cols - a.shape[1])))


def _hyena_filters(L, li, hy_w1, hy_b1, hy_fr1, hy_w2, hy_b2, hy_fr2, hy_w3, f_tab, hy_ch):
    bands = (HY_EMB - 1) // 2
    pos = jnp.arange(L, dtype=F32)
    t = jnp.linspace(0.0, 1.0, L, dtype=F32)
    f = jnp.linspace(1e-4, bands - 1, bands, dtype=F32)
    ang = 2.0 * math.pi * pos[:, None] * f[None, :] / L
    z = jnp.concatenate([t[:, None], jnp.cos(ang), -jnp.sin(ang)], axis=-1)
    z = _pad2(z, L, LANES)
    max_decay = math.log(HY_TARGET) / HY_FAST_PCT
    min_decay = math.log(HY_TARGET) / HY_SLOW_PCT
    deltas = jnp.abs(jnp.linspace(min_decay, max_decay, hy_ch, dtype=F32)).reshape(1, hy_ch)
    n_even = hy_w1.shape[0]
    w1 = jnp.pad(hy_w1, ((0, 0), (0, LANES - HY_EMB), (0, LANES - HY_FILT_HID)))
    w2 = jnp.pad(hy_w2, ((0, 0), (0, LANES - HY_FILT_HID), (0, LANES - HY_FILT_HID)))
    w3 = jnp.pad(hy_w3, ((0, 0), (0, LANES - HY_FILT_HID), (0, 0)))
    padv = lambda a: _pad2(a[li:li + 1], 1, LANES)
    cb = min(256, hy_ch)
    ncb = hy_ch // cb
    vec = pl.BlockSpec((1, LANES), lambda j: (0, 0))
    sq = pl.BlockSpec((1, LANES, LANES), lambda j: (li, 0, 0))
    return pl.pallas_call(
        functools.partial(_filter_body, L=L),
        grid=(2 * ncb,),
        in_specs=[pl.BlockSpec((L, LANES), lambda j: (0, 0)),
                  sq, vec, vec, sq, vec, vec,
                  pl.BlockSpec((1, LANES, cb), lambda j: (li, 0, j)),
                  pl.BlockSpec((1, LANES, cb), lambda j: (li, 0, 2 * ncb + j)),
                  pl.BlockSpec((1, cb), lambda j: (0, j % ncb)),
                  pl.BlockSpec((2 * L, L), lambda j: (0, 0))],
        out_specs=pl.BlockSpec((1, 2 * L, cb), lambda j: (j // ncb, 0, j % ncb)),
        out_shape=_sds((2, 2 * L, hy_ch), F32),
        scratch_shapes=[pltpu.VMEM((L, LANES), F32)],
        compiler_params=_cp(("arbitrary",), big=True),
        name="hyena_filters",
    )(z, w1, padv(hy_b1), padv(hy_fr1), w2, padv(hy_b2), padv(hy_fr2), w3, w3, deltas, f_tab)


def _hyena_body(ymix_ref, v_ref, x1_ref, x2_ref, swv_ref, sw1_ref, sw2_ref, sbv_ref, sb1_ref,
                sb2_ref, fb_ref, hs_ref, f_ref, g_ref, o_ref, *, L, spb):
    del ymix_ref
    row = lax.broadcasted_iota(jnp.int32, (L, 1), 0)

    def short(x_ref, w_ref, b_ref, r0):
        x = x_ref[r0:r0 + L, :]
        w = w_ref[0]
        xm = jnp.where(row == 0, 0.0, pltpu.roll(x, 1, 0))
        xp = jnp.where(row == L - 1, 0.0, pltpu.roll(x, L - 1, 0))
        return xm * w[0:1] + x * w[1:2] + xp * w[2:3] + b_ref[0]

    def fconv(z, order):
        zf = jnp.dot(f_ref[...], z.astype(BF16), preferred_element_type=F32)
        p, q = zf[0:L], zf[L:2 * L]
        hp, hq = hs_ref[order, 0:L, :], hs_ref[order, L:2 * L, :]
        yr = jnp.where(row == 0, p * hp, p * hp - q * hq)
        yq = jnp.where(row == 0, q * hq, p * hq + q * hp)
        return (jnp.dot(g_ref[:, 0:L], yr.astype(BF16), preferred_element_type=F32)
                + jnp.dot(g_ref[:, L:2 * L], yq.astype(BF16), preferred_element_type=F32))

    fb = fb_ref[0]
    for s in range(spb):
        r0 = s * L
        v = short(v_ref, swv_ref, sbv_ref, r0)
        x1 = short(x1_ref, sw1_ref, sb1_ref, r0)
        x2 = short(x2_ref, sw2_ref, sb2_ref, r0)
        z1 = x1 * (fconv(v, 0) + fb[0:1] * v)
        o_ref[r0:r0 + L, :] = (x2 * (fconv(z1, 1) + fb[1:2] * z1)).astype(o_ref.dtype)


def _hyena(proj, ymix, li, nseq, L, row_blk0, hy_short_w, hy_short_b, hy_fbias, hs, f_bf, g_bf,
           hy_ch):
    t, d = ymix.shape
    cb = min(256, hy_ch)
    ncb = hy_ch // cb
    spb = 2 if nseq % 2 == 0 and row_blk0 % 2 == 0 else 1
    rb0 = row_blk0 // spb
    rowmap = lambda c: (lambda j, b: (rb0 + b, c * ncb + j))
    wmap = lambda c: (lambda j, b: (li, 0, c * ncb + j))
    sb = hy_short_b.reshape(hy_short_b.shape[0], 1, -1)
    in_specs = [pl.BlockSpec(memory_space=pl.ANY),
                pl.BlockSpec((spb * L, cb), rowmap(0)), pl.BlockSpec((spb * L, cb), rowmap(1)),
                pl.BlockSpec((spb * L, cb), rowmap(2)),
                pl.BlockSpec((1, 3, cb), wmap(0)), pl.BlockSpec((1, 3, cb), wmap(1)),
                pl.BlockSpec((1, 3, cb), wmap(2)),
                pl.BlockSpec((1, 1, cb), wmap(0)), pl.BlockSpec((1, 1, cb), wmap(1)),
                pl.BlockSpec((1, 1, cb), wmap(2)),
                pl.BlockSpec((1, 2, cb), lambda j, b: (li, 0, j)),
                pl.BlockSpec((2, 2 * L, cb), lambda j, b: (0, 0, j)),
                pl.BlockSpec((2 * L, L), lambda j, b: (0, 0)),
                pl.BlockSpec((L, 2 * L), lambda j, b: (0, 0))]
    args = [ymix, proj, proj, proj, hy_short_w, hy_short_w, hy_short_w, sb, sb, sb, hy_fbias, hs,
            f_bf, g_bf]
    return pl.pallas_call(
        functools.partial(_hyena_body, L=L, spb=spb),
        grid=(ncb, nseq // spb),
        in_specs=in_specs,
        out_specs=pl.BlockSpec((spb * L, cb), lambda j, b: (rb0 + b, j)),
        out_shape=_sds((t, d), BF16),
        input_output_aliases={0: 0},
        compiler_params=_cp(("arbitrary", "arbitrary"), big=True),
        name="hyena_mixer",
    )(*args)


def _rope_tables(L):
    half = DA_DH // 2
    nf = half // 2
    lane = jnp.arange(LANES)
    sub = lane % DA_DH
    idx = sub % half
    inv = ROPE_THETA ** (-(idx % nf).astype(F32) / nf)
    pos = jnp.arange(L)
    p = jnp.where((sub // half)[None, :] == 0, (pos // GRID_W)[:, None], (pos % GRID_W)[:, None])
    ang = p.astype(F32) * inv[None, :]
    sgn = jnp.where(idx < nf, -1.0, 1.0)[None, :]
    return jnp.cos(ang), jnp.sin(ang) * sgn


def _attn_body(*refs, li, cached, tq):
    it = iter(refs)
    next(it)
    lq1, lk1, lq2, lk2 = [next(it) for _ in range(4)]
    q_ref, k_ref, v_ref = next(it), next(it), next(it)
    if cached:
        ck_ref, cv_ref, cq_ref, sq_ref, ckk_ref, skk_ref = [next(it) for _ in range(6)]
    g_ref = next(it)
    o_ref = next(it)

    lam = (jnp.exp(jnp.sum(lq1[...] * lk1[...], keepdims=True))
           - jnp.exp(jnp.sum(lq2[...] * lk2[...], keepdims=True)) + li)
    lane = lax.broadcasted_iota(jnp.int32, (1, LANES), 1)
    first = lane < DA_DH
    nf = DA_DH // 4

    def rope(x, c_ref, s_ref):
        swap = jnp.where((lane % (2 * nf)) < nf, pltpu.roll(x, LANES - nf, 1), pltpu.roll(x, nf, 1))
        return x * c_ref[...] + swap * s_ref[...]

    q = q_ref[...]
    k = k_ref[...]
    if cached:
        q = rope(q, cq_ref, sq_ref)
        k = rope(k, ckk_ref, skk_ref)
    q = q * (DA_DH ** -0.5)
    qs = [jnp.where(first, q, 0.0).astype(BF16), jnp.where(first, 0.0, q).astype(BF16)]
    nt = (((1,), (1,)), ((), ()))
    kb = k.astype(BF16)
    vb = v_ref[...].astype(BF16)
    if cached:
        ckb = ck_ref[0, 0].astype(BF16)
        cvb = cv_ref[0, 0].astype(BF16)

    def probs(qm):
        s = lax.dot_general(qm, kb, nt, preferred_element_type=F32)
        m = jnp.max(s, axis=-1, keepdims=True)
        if cached:
            sc = lax.dot_general(qm, ckb, nt, preferred_element_type=F32)
            m = jnp.maximum(m, jnp.max(sc, axis=-1, keepdims=True))
            ec = jnp.exp(sc - m)
        e = jnp.exp(s - m)
        den = jnp.sum(e, axis=-1, keepdims=True)
        if cached:
            den = den + jnp.sum(ec, axis=-1, keepdims=True)
            return e / den, ec / den
        return e / den, None

    p1, pc1 = probs(qs[0])
    p2, pc2 = probs(qs[1])
    o = jnp.dot((p1 - lam * p2).astype(BF16), vb, preferred_element_type=F32)
    if cached:
        o = o + jnp.dot((pc1 - lam * pc2).astype(BF16), cvb, preferred_element_type=F32)
    o = _rms(o, g_ref[0]) * (1.0 - li)
    o_ref[...] = o.astype(o_ref.dtype)


def _attention(proj, ymix, li, layer, nseq, L, row0, lams, da_norm_g, hy_ch, da_width,
               cache=None):
    t, d = ymix.shape
    heads = da_width // DA_VD
    cached = cache is not None
    tq = min(512, L)
    nq = L // tq
    qc0 = 3 * hy_ch // LANES
    kc0 = qc0 + da_width // LANES
    vc0 = kc0 + da_width // LANES
    lam_spec = pl.BlockSpec((1, DA_DH), lambda b, h, qi: (0, 0))
    in_specs = [pl.BlockSpec(memory_space=pl.ANY)] + [lam_spec] * 4 + [
        pl.BlockSpec((tq, LANES), lambda b, h, qi: (row0 // tq + b * nq + qi, qc0 + h)),
        pl.BlockSpec((L, LANES), lambda b, h, qi: (row0 // L + b, kc0 + h)),
        pl.BlockSpec((L, LANES), lambda b, h, qi: (row0 // L + b, vc0 + h))]
    args = [ymix] + [a[li:li + 1] for a in lams] + [proj, proj, proj]
    if cached:
        ck, cv, cos_t, sin_t = cache
        past = ck.shape[2]
        in_specs += [pl.BlockSpec((1, 1, past, LANES), lambda b, h, qi: (b, li, 0, h)),
                     pl.BlockSpec((1, 1, past, LANES), lambda b, h, qi: (b, li, 0, h)),
                     pl.BlockSpec((tq, LANES), lambda b, h, qi: (qi, 0)),
                     pl.BlockSpec((tq, LANES), lambda b, h, qi: (qi, 0)),
                     pl.BlockSpec((L, LANES), lambda b, h, qi: (0, 0)),
                     pl.BlockSpec((L, LANES), lambda b, h, qi: (0, 0))]
        args += [ck, cv, cos_t, sin_t, cos_t, sin_t]
    in_specs.append(pl.BlockSpec((1, LANES), lambda b, h, qi: (0, 0)))
    args.append(da_norm_g[li:li + 1])
    lam_init = 0.8 - 0.6 * math.exp(-0.3 * layer)
    return pl.pallas_call(
        functools.partial(_attn_body, li=lam_init, cached=cached, tq=tq),
        grid=(nseq, heads, nq),
        in_specs=in_specs,
        out_specs=pl.BlockSpec((tq, LANES),
                               lambda b, h, qi: (row0 // tq + b * nq + qi, hy_ch // LANES + h)),
        out_shape=_sds((t, d), BF16),
        input_output_aliases={0: 0},
        compiler_params=_cp(("arbitrary", "arbitrary", "arbitrary"), big=True),
        name="diff_attention",
    )(*args)


def _dw_tables_body(f_ref, g_ref, b_ref, *, rt, kp, taps):
    n = 2 * rt
    r = lax.broadcasted_iota(jnp.int32, (n, kp), 0)
    j = lax.broadcasted_iota(jnp.int32, (n, kp), 1)
    f_ref[...] = _dft_basis(r, j, rt)
    t = lax.broadcasted_iota(jnp.int32, (rt, n), 0)
    r2 = lax.broadcasted_iota(jnp.int32, (rt, n), 1)
    scale = jnp.where((r2 == 0) | (r2 == rt), 1.0 / n, 2.0 / n)
    shift = taps - 1 + HALO - taps // 2
    g_ref[...] = _dft_basis(r2, t + shift, rt) * scale
    r3 = lax.broadcasted_iota(jnp.int32, (n, LANES), 0)
    k3 = lax.broadcasted_iota(jnp.int32, (n, LANES), 1)
    b_ref[...] = _dft_basis(r3, (taps - 1 - k3) & (n - 1), rt)


def _dw_tables(rt, taps):
    kp = -(-(rt + 2 * HALO) // LANES) * LANES
    return pl.pallas_call(
        functools.partial(_dw_tables_body, rt=rt, kp=kp, taps=taps),
        out_shape=[_sds((2 * rt, kp), F32), _sds((rt, 2 * rt), F32), _sds((2 * rt, LANES), F32)],
        name="dwconv_tables",
    )()


def _dwconv_body(prev_ref, cur_ref, next_ref, w_ref, b_ref, lg_ref, lb_ref, f_ref, g_ref, bm_ref,
                 o_ref, buf, hd, *, rt, dims):
    t_ctx, l_ctx, l_lat = dims
    i = pl.program_id(0)

    @pl.when(i == 0)
    def _():
        hd[...] = jnp.dot(bm_ref[...], w_ref[0], precision=HIGHEST, preferred_element_type=F32)
        buf[rt + 2 * HALO:, :] = jnp.zeros((buf.shape[0] - rt - 2 * HALO, buf.shape[1]), F32)

    row0 = i * rt
    in_ctx = row0 < t_ctx
    seq_pos = jnp.where(in_ctx, row0 % l_ctx, (row0 - t_ctx) % l_lat)
    seq_len = jnp.where(in_ctx, l_ctx, l_lat)
    is_first = seq_pos == 0
    is_last = seq_pos + rt == seq_len
    buf[0:HALO, :] = jnp.where(is_first, 0.0, prev_ref[...])
    buf[HALO:HALO + rt, :] = cur_ref[...]
    buf[HALO + rt:2 * HALO + rt, :] = jnp.where(is_last, 0.0, next_ref[...])

    z = jnp.dot(f_ref[...], buf[...].astype(BF16), preferred_element_type=F32)
    p, q = z[0:rt], z[rt:2 * rt]
    hp, hq = hd[0:rt, :], hd[rt:2 * rt, :]
    row = lax.broadcasted_iota(jnp.int32, (rt, 1), 0)
    yr = jnp.where(row == 0, p * hp, p * hp - q * hq)
    yq = jnp.where(row == 0, q * hq, p * hq + q * hp)
    u = (jnp.dot(g_ref[:, 0:rt], yr.astype(BF16), preferred_element_type=F32)
         + jnp.dot(g_ref[:, rt:2 * rt], yq.astype(BF16), preferred_element_type=F32) + b_ref[0])
    mu = jnp.mean(u, axis=-1, keepdims=True)
    uc = u - mu
    var = jnp.mean(uc * uc, axis=-1, keepdims=True)
    y = uc * lax.rsqrt(var + EPS) * lg_ref[0] + lb_ref[0]
    o_ref[...] = (y * jax.nn.sigmoid(y)).astype(o_ref.dtype)


def _conformer_dwconv(u, li, cv_dw, cv_dwb, cv_ln_g, cv_ln_b, tabs, dims, rt):
    t, d = u.shape
    f_bf, g_bf, bm = tabs
    kp = f_bf.shape[1]
    hb = rt // HALO
    nhb = t // HALO
    w_pad = jnp.pad(cv_dw, ((0, 0), (0, LANES - cv_dw.shape[1]), (0, 0)))
    vec = lambda a: a.reshape(a.shape[0], 1, d)
    vspec = pl.BlockSpec((1, 1, d), lambda i: (li, 0, 0))
    full = lambda a: pl.BlockSpec(a.shape, lambda i: (0, 0))
    return pl.pallas_call(
        functools.partial(_dwconv_body, rt=rt, dims=dims),
        grid=(t // rt,),
        in_specs=[pl.BlockSpec((HALO, d), lambda i: (jnp.maximum(i * hb - 1, 0), 0)),
                  pl.BlockSpec((rt, d), lambda i: (i, 0)),
                  pl.BlockSpec((HALO, d), lambda i: (jnp.minimum((i + 1) * hb, nhb - 1), 0)),
                  pl.BlockSpec((1, LANES, d), lambda i: (li, 0, 0)),
                  vspec, vspec, vspec, full(f_bf), full(g_bf), full(bm)],
        out_specs=pl.BlockSpec((rt, d), lambda i: (i, 0)),
        out_shape=_sds((t, d), BF16),
        scratch_shapes=[pltpu.VMEM((kp, d), F32), pltpu.VMEM((2 * rt, d), F32)],
        compiler_params=_cp(("arbitrary",), big=True),
        name="conformer_dwconv",
    )(u, u, u, w_pad, vec(cv_dwb), vec(cv_ln_g), vec(cv_ln_b), f_bf, g_bf, bm)


def _scatter_body(pos_ref, pad_ref, h_ref, xs_ref, zrow, sem, zsem, *, rt, n_exp):
    i = pl.program_id(0)

    def issue(r, carry):
        for kk in range(TOP_K):
            p = pos_ref[(i * rt + r) * TOP_K + kk]
            pltpu.make_async_copy(h_ref.at[pl.ds(r, 1)], xs_ref.at[pl.ds(p, 1)], sem).start()
        return carry

    lax.fori_loop(0, rt, issue, 0, unroll=DMA_UNROLL)

    @pl.when(i == 0)
    def _():
        zrow[...] = jnp.zeros(zrow.shape, zrow.dtype)
        for e in range(n_exp):
            lo, hi = pad_ref[e], pad_ref[n_exp + e]

            def zissue(r, carry):
                pltpu.make_async_copy(zrow, xs_ref.at[pl.ds(r, 1)], zsem).start()
                return carry

            def zwait(r, carry):
                pltpu.make_async_copy(zrow, xs_ref.at[pl.ds(r, 1)], zsem).wait()
                return carry

            lax.fori_loop(lo, hi, zissue, 0)
            lax.fori_loop(lo, hi, zwait, 0)

    def drain(r, carry):
        for kk in range(TOP_K):
            pltpu.make_async_copy(h_ref.at[pl.ds(0, 1)], xs_ref.at[pl.ds(0, 1)], sem).wait()
        return carry

    lax.fori_loop(0, rt, drain, 0, unroll=DMA_UNROLL)


def _moe_scatter(h, pos, pad_ranges, n_rows, n_exp, rt):
    t, d = h.shape
    grid_spec = pltpu.PrefetchScalarGridSpec(
        num_scalar_prefetch=2, grid=(t // rt,),
        in_specs=[pl.BlockSpec((rt, d), lambda i, p, z: (i, 0))],
        out_specs=pl.BlockSpec(memory_space=pl.ANY),
        scratch_shapes=[pltpu.VMEM((1, d), h.dtype), pltpu.SemaphoreType.DMA,
                        pltpu.SemaphoreType.DMA])
    return pl.pallas_call(
        functools.partial(_scatter_body, rt=rt, n_exp=n_exp),
        grid_spec=grid_spec, out_shape=_sds((n_rows, d), h.dtype),
        compiler_params=_cp(("arbitrary",)),
        name="moe_scatter",
    )(pos, pad_ranges, h)


def _combine_body(pos_ref, x_ref, meta_ref, gate_ref, ys_ref, *rest, rt, post):
    if post == "norm_mod":
        g_ref, sh_ref, sc_ref, o_ref, h_ref, buf, sem = rest
    elif post == "final":
        g_ref, o_ref, buf, sem = rest
    else:
        o_ref, buf, sem = rest
    i = pl.program_id(0)

    def issue(r, carry):
        for kk in range(TOP_K):
            p = pos_ref[(i * rt + r) * TOP_K + kk]
            pltpu.make_async_copy(ys_ref.at[pl.ds(p, 1)], buf.at[kk, pl.ds(r, 1)], sem).start()
        return carry

    lax.fori_loop(0, rt, issue, 0, unroll=DMA_UNROLL)

    def drain(r, carry):
        for kk in range(TOP_K):
            pltpu.make_async_copy(ys_ref.at[pl.ds(0, 1)], buf.at[kk, pl.ds(0, 1)], sem).wait()
        return carry

    lax.fori_loop(0, rt, drain, 0, unroll=DMA_UNROLL)
    meta = meta_ref[...]
    f = meta[:, 2:3] * buf[0] + meta[:, 3:4] * buf[1]
    x_new = x_ref[...] + gate_ref[0] * f
    if post == "final":
        o_ref[...] = _rms(x_new, g_ref[...])
        return
    o_ref[...] = x_new
    if post == "norm_mod":
        h_ref[...] = (_rms(x_new, g_ref[0]) * (1.0 + sc_ref[0]) + sh_ref[0]).astype(h_ref.dtype)


def _moe_combine(x, ys, pos, meta, modr, layer, chunk, dims, rt, post=None, post_args=None):
    t, d = x.shape
    t_ctx, l_lat = dims
    mod_spec = lambda lyr, ch: pl.BlockSpec((1, 1, d), lambda i, p: (
        lyr * MOD_ROWS + _mod_row(i * rt, t_ctx, l_lat), 0, ch))
    row_spec = pl.BlockSpec((rt, d), lambda i, p: (i, 0))
    in_specs = [row_spec, pl.BlockSpec((rt, LANES), lambda i, p: (i, 0)), mod_spec(layer, chunk),
                pl.BlockSpec(memory_space=pl.ANY)]
    args = [pos, x, meta, modr, ys]
    out_specs, out_shape = row_spec, _sds((t, d), F32)
    if post == "norm_mod":
        g_all, nxt = post_args
        in_specs += [pl.BlockSpec((1, 1, d), lambda i, p: (nxt, 0, 0)), mod_spec(nxt, 0),
                     mod_spec(nxt, 1)]
        args += [g_all.reshape(-1, 1, d), modr, modr]
        out_specs, out_shape = [row_spec, row_spec], [_sds((t, d), F32), _sds((t, d), BF16)]
    elif post == "final":
        in_specs.append(pl.BlockSpec((1, d), lambda i, p: (0, 0)))
        args.append(post_args.reshape(1, d))
    grid_spec = pltpu.PrefetchScalarGridSpec(
        num_scalar_prefetch=1, grid=(t // rt,), in_specs=in_specs, out_specs=out_specs,
        scratch_shapes=[pltpu.VMEM((TOP_K, rt, d), F32), pltpu.SemaphoreType.DMA])
    return pl.pallas_call(
        functools.partial(_combine_body, rt=rt, post=post),
        grid_spec=grid_spec, out_shape=out_shape,
        compiler_params=_cp(("arbitrary",)),
        name="moe_combine",
    )(*args)


def _routing_tables(meta, n_exp, tg, n_rows):
    ids = meta[:, 0:TOP_K].astype(jnp.int32).reshape(-1)
    onehot = (ids[:, None] == jnp.arange(n_exp)[None, :]).astype(jnp.int32)
    csum = jnp.cumsum(onehot, axis=0)
    counts = csum[-1]
    rank = jnp.sum(onehot * csum, axis=1) - 1
    padded = (counts + tg - 1) // tg * tg
    ends = jnp.cumsum(padded)
    starts = ends - padded
    pos = (starts[ids] + rank).astype(jnp.int32)
    hi = ends.at[-1].set(n_rows)
    pad_ranges = jnp.concatenate([starts + counts, hi]).astype(jnp.int32)
    return pos, pad_ranges, (starts + counts, ends)


def _tile_tables(geometry, tm, n_rows, base):
    real_end, ends = geometry
    tile_start = jnp.arange(n_rows // tm, dtype=jnp.int32) * tm
    te = jnp.sum((tile_start[:, None] >= ends[None, :]).astype(jnp.int32), axis=1)
    te_last = jnp.sum((jnp.maximum(ends[-1] - tm, 0) >= ends).astype(jnp.int32))
    active = tile_start < ends[-1]
    te = jnp.where(active, te, te_last).astype(jnp.int32)
    chunk = tm // ROW_CHUNKS
    real = jnp.clip(real_end[te] - tile_start, 0, tm)
    nv = jnp.where(active, (real + chunk - 1) // chunk, 0).astype(jnp.int32)
    n_active = (ends[-1] // tm).astype(jnp.int32).reshape(1)
    return te, nv, n_active, base


def kernel(x_prompt, x_sample, cache_k, cache_v, c, c_ctx, norm_mix_g, norm_ffn_g, w_mod, b_mod,
           w_in, w_out, hy_short_w, hy_short_b, hy_w1, hy_b1, hy_fr1, hy_w2, hy_b2, hy_fr2, hy_w3,
           hy_fbias, da_lq1, da_lk1, da_lq2, da_lk2, da_norm_g, cv_w1, cv_b1, cv_dw, cv_dwb,
           cv_ln_g, cv_ln_b, cv_w2, cv_b2, ff_w1, ff_w3, ff_w2, moe_router, moe_w1, moe_w3, moe_w2,
           final_g):
    b_ctx, l_ctx, d = x_prompt.shape
    b_lat, l_lat, _ = x_sample.shape
    t_ctx, t_lat = b_ctx * l_ctx, b_lat * l_lat
    t = t_ctx + t_lat
    depth = w_mod.shape[0]
    hy_ch = hy_fbias.shape[-1]
    da_width = d - hy_ch
    heads = da_width // DA_VD
    n_exp = moe_router.shape[-1]
    d_ff = ff_w1.shape[-1]
    e_ff = moe_w1.shape[-1]
    past = cache_k.shape[2]
    n_even = w_in.shape[0]
    dims = (t_ctx, l_lat)
    rt = min(256, l_ctx)
    tm_big = min(1024, l_lat, t_ctx)
    tm_res = min(512, l_lat, t_ctx)
    tm_up = min(512, t)
    tm_down = min(512, t)
    assert b_lat + 1 <= MOD_ROWS and t_ctx % tm_big == 0 and l_lat % tm_big == 0

    x = jnp.concatenate([x_prompt.reshape(t_ctx, d), x_sample.reshape(t_lat, d)], axis=0)
    cond = jnp.zeros((MOD_ROWS, d), F32).at[0].set(c_ctx).at[1:1 + b_lat].set(c)
    modr = _mods(cond, w_mod, b_mod).reshape(depth * MOD_ROWS, 1, 6 * d)

    tables = {}
    for L in sorted({l_ctx, l_lat}):
        f_tab, g_tab = _dft_tables(L)
        tables[L] = (f_tab, f_tab.astype(BF16), g_tab.astype(BF16))
    dw_f, dw_g, dw_b = _dw_tables(rt, cv_dw.shape[1])
    dw_tabs = (dw_f.astype(BF16), dw_g.astype(BF16), dw_b)
    cos_t, sin_t = _rope_tables(l_lat)
    ck = cache_k.reshape(b_lat, n_even, past, heads * DA_VD)
    cv = cache_v.reshape(b_lat, n_even, past, heads * DA_VD)
    lams = (da_lq1, da_lk1, da_lq2, da_lk2)
    router_pad = jnp.pad(moe_router, ((0, 0), (0, 0), (0, LANES - n_exp)))
    moe_w1r = moe_w1.reshape(-1, d, e_ff)
    moe_w3r = moe_w3.reshape(-1, d, e_ff)
    moe_w2r = moe_w2.reshape(-1, e_ff, d)
    n_rows = ((t * TOP_K) // tm_up + n_exp) * tm_up
    vec3 = lambda a: a.reshape(a.shape[0], 1, a.shape[-1])

    ks, vs = [], []
    h_next = None
    for layer in range(depth):
        li = layer // 2
        if layer % 2 == 0:
            h = h_next if h_next is not None else _norm_mod(x, norm_mix_g, layer, modr, 0, 1,
                                                            dims, tm_res)
            h_next = None
            proj = _matmul(h, [w_in], li, [0], w_in.shape[-1], tm=tm_big,
                           tn=min(1024, w_in.shape[-1]), name="in_proj")
            ymix = jnp.zeros((t, d), BF16)
            for (nseq, L, row0) in ((b_ctx, l_ctx, 0), (b_lat, l_lat, t_ctx)):
                f_tab, f_bf, g_bf = tables[L]
                hs = _hyena_filters(L, li, hy_w1, hy_b1, hy_fr1, hy_w2, hy_b2, hy_fr2, hy_w3,
                                    f_tab, hy_ch)
                ymix = _hyena(proj, ymix, li, nseq, L, row0 // L, hy_short_w, hy_short_b,
                              hy_fbias, hs, f_bf, g_bf, hy_ch)
            ymix = _attention(proj, ymix, li, layer, b_ctx, l_ctx, 0, lams, da_norm_g, hy_ch,
                              da_width)
            ymix = _attention(proj, ymix, li, layer, b_lat, l_lat, t_ctx, lams, da_norm_g, hy_ch,
                              da_width, cache=(ck, cv, cos_t, sin_t))
            kc0 = 3 * hy_ch + da_width
            ks.append(proj[:t_ctx, kc0:kc0 + da_width].reshape(b_ctx, l_ctx, heads, DA_VD))
            vs.append(proj[:t_ctx, kc0 + da_width:kc0 + 2 * da_width]
                      .reshape(b_ctx, l_ctx, heads, DA_VD))
            x = _matmul(ymix, [w_out], li, [0], d, tm=tm_big, tn=min(1024, d), epi="plain",
                        res=x, gate=(modr, layer, 2, dims), name="out_proj")
            h = _norm_mod(x, norm_ffn_g, layer, modr, 3, 4, dims, tm_res)
            hid = _matmul(h, [ff_w1, ff_w3], li, [0, 0], d_ff, tm=tm_big, tn=min(512, d_ff),
                          epi="swiglu", out_dtype=BF16, name="ffn_up")
            x = _matmul(hid, [ff_w2], li, [0], d, tm=tm_res, tn=min(512, d), res=x,
                        gate=(modr, layer, 5, dims), name="ffn_down")
        else:
            h = _norm_mod(x, norm_mix_g, layer, modr, 0, 1, dims, tm_res)
            u = _matmul(h, [cv_w1, cv_w1], li, [0, d], d, tm=tm_big, tn=min(512, d), epi="glu",
                        biases=[vec3(cv_b1), vec3(cv_b1)], bidx=li, name="conformer_up")
            u = _conformer_dwconv(u, li, cv_dw, cv_dwb, cv_ln_g, cv_ln_b, dw_tabs,
                                  (t_ctx, l_ctx, l_lat), rt)
            x = _matmul(u, [cv_w2], li, [0], d, tm=tm_big, tn=min(1024, d), biases=[vec3(cv_b2)],
                        bidx=li, res=x, gate=(modr, layer, 2, dims), name="conformer_down")
            hpk, meta = _norm_router(x, norm_ffn_g, layer, modr, 3, 4, router_pad[li], n_exp,
                                     dims, tm_res)
            pos, pad_ranges, geometry = _routing_tables(meta, n_exp, tm_up, n_rows)
            xs = _moe_scatter(hpk, pos, pad_ranges, n_rows, n_exp, rt)
            group = _tile_tables(geometry, tm_up, n_rows, li * n_exp)
            hid = _matmul(xs, [moe_w1r, moe_w3r], 0, [0, 0], e_ff, tm=tm_up,
                          tn=min(1024, e_ff), epi="swiglu", out_dtype=BF16, group=group,
                          name="moe_up")
            ys = _matmul(hid, [moe_w2r], 0, [0], d, tm=tm_down, tn=min(512, d),
                         group=_tile_tables(geometry, tm_down, n_rows, li * n_exp),
                         name="moe_down")
            if layer + 1 < depth and (layer + 1) % 2 == 0:
                x, h_next = _moe_combine(x, ys, pos, meta, modr, layer, 5, dims, rt,
                                         post="norm_mod", post_args=(norm_mix_g, layer + 1))
            elif layer + 1 == depth:
                y = _moe_combine(x, ys, pos, meta, modr, layer, 5, dims, rt, post="final",
                                 post_args=final_g)
            else:
                x = _moe_combine(x, ys, pos, meta, modr, layer, 5, dims, rt)

    if depth % 2 == 1:
        y = _final_norm(x, final_g, tm_res)
    y_prompt = y[:t_ctx].reshape(b_ctx, l_ctx, d)
    y_sample = y[t_ctx:].reshape(b_lat, l_lat, d)
    return (y_prompt, y_sample, jnp.stack(ks, axis=1), jnp.stack(vs, axis=1))
```
